```python
import jax, jax.numpy as jnp
from jax import lax
import numpy as np

D_MODEL = 2048
BATCH = 16
SEQ = 256
DEPTH = 1
DEC_BATCH = 2
DEC_SEQ = 4096
PAST_LEN = 256

GRID_W = 64
D_MIX = D_MODEL
N_HEADS = 8
KV_HEADS = 2
HEAD_DIM = 128
GQA_GROUP = N_HEADS // KV_HEADS
ATTN_W = N_HEADS * HEAD_DIM
KV_W = KV_HEADS * HEAD_DIM
G_HEADS = 8
G_DIM = (D_MIX - ATTN_W) // G_HEADS
GMLP_W = G_HEADS * G_DIM
IN_W = ATTN_W + 2 * KV_W + 2 * GMLP_W
CHUNK = 128
Q_BLOCK = 128
ROPE_THETA = 10000.0
AXIS_DIM = HEAD_DIM // 2
N_EXPERTS = 32
TOP_K = 4
D_FF = D_MODEL
SWIGLU_ALPHA = 1.702
SWIGLU_LIMIT = 7.0
MOE_BLOCK = 128
N_MOD = 6
EPS = 1e-6

kernel_name = 'hymba_gqa_gmlp_moe_adaln_diffusion_step'


def rms_normalize(x):
    xf = x.astype(jnp.float32)
    return (xf * lax.rsqrt(jnp.mean(xf * xf, axis=-1, keepdims=True) + EPS)).astype(x.dtype)


def modulation(cvec, w_mod, b_mod):
    m = jax.nn.silu(cvec) @ w_mod + b_mod
    return [p[:, None, :] for p in jnp.split(m, N_MOD, axis=-1)]


def modulate(x, gain, shift, scale):
    return rms_normalize(x) * gain * (1.0 + scale) + shift


def axial_rope_tables(n_tokens):
    n_rows = n_tokens // GRID_W
    row = jnp.repeat(jnp.arange(n_rows, dtype=jnp.float32), GRID_W)
    col = jnp.tile(jnp.arange(GRID_W, dtype=jnp.float32), n_rows)
    half = AXIS_DIM // 2
    inv_freq = ROPE_THETA ** (-jnp.arange(half, dtype=jnp.float32) / half)
    ang_r = row[:, None] * inv_freq[None, :]
    ang_c = col[:, None] * inv_freq[None, :]
    return (jnp.cos(ang_r), jnp.sin(ang_r), jnp.cos(ang_c), jnp.sin(ang_c))


def rotate_axis(x, cos, sin):
    x1, x2 = x[..., :AXIS_DIM // 2], x[..., AXIS_DIM // 2:]
    cos = cos[None, :, None, :]
    sin = sin[None, :, None, :]
    return jnp.concatenate([x1 * cos - x2 * sin, x2 * cos + x1 * sin], axis=-1)


def apply_axial_rope(x, tables):
    cos_r, sin_r, cos_c, sin_c = tables
    xf = x.astype(jnp.float32)
    out = jnp.concatenate([rotate_axis(xf[..., :AXIS_DIM], cos_r, sin_r),
                           rotate_axis(xf[..., AXIS_DIM:], cos_c, sin_c)], axis=-1)
    return out.astype(x.dtype)


def attend(q, k, v):
    B, S = q.shape[0], q.shape[1]
    nb = S // Q_BLOCK
    qb = q.reshape(B, nb, Q_BLOCK, KV_HEADS, GQA_GROUP, HEAD_DIM).transpose(1, 0, 2, 3, 4, 5)
    scale = HEAD_DIM ** -0.5

    def one_block(q_blk):
        s = jnp.einsum('bqkgd,blkd->bkgql', q_blk, k).astype(jnp.float32) * scale
        p = jax.nn.softmax(s, axis=-1).astype(v.dtype)
        return jnp.einsum('bkgql,blkd->bqkgd', p, v)

    o = lax.map(one_block, qb)
    return o.transpose(1, 0, 2, 3, 4, 5).reshape(B, S, ATTN_W)


def chunk_mlp(u, g, w_sp, b_sp):
    B, S = u.shape[0], u.shape[1]
    nc = S // CHUNK
    gc = rms_normalize(g).reshape(B, nc, CHUNK, G_HEADS, G_DIM)
    sp = jnp.einsum('hpq,bnqhd->bnphd', w_sp, gc) + b_sp.T[None, None, :, :, None]
    return (u * sp.reshape(B, S, G_HEADS, G_DIM)).reshape(B, S, GMLP_W)


def mixer(h, k_ctx, v_ctx, tables, w_in, q_gain, k_gain, w_sp, b_sp, attn_out_gain, gmlp_out_gain, w_out):
    B, S, _ = h.shape
    z = jnp.einsum('bsd,de->bse', h, w_in)
    q, k, v, u, g = jnp.split(z, [ATTN_W, ATTN_W + KV_W, ATTN_W + 2 * KV_W, ATTN_W + 2 * KV_W + GMLP_W], axis=-1)
    q = rms_normalize(q.reshape(B, S, N_HEADS, HEAD_DIM)) * q_gain
    k = rms_normalize(k.reshape(B, S, KV_HEADS, HEAD_DIM)) * k_gain
    v = v.reshape(B, S, KV_HEADS, HEAD_DIM)
    u = jax.nn.gelu(u).reshape(B, S, G_HEADS, G_DIM)
    g = jax.nn.gelu(g).reshape(B, S, G_HEADS, G_DIM)
    if tables is not None:
        q = apply_axial_rope(q, tables)
        k = apply_axial_rope(k, tables)
    if k_ctx is not None:
        k_all = jnp.concatenate([k_ctx, k], axis=1)
        v_all = jnp.concatenate([v_ctx, v], axis=1)
    else:
        k_all, v_all = k, v
    attn = attend(q, k_all, v_all)
    gm = chunk_mlp(u, g, w_sp, b_sp)
    merged = jnp.concatenate([rms_normalize(attn) * attn_out_gain, rms_normalize(gm) * gmlp_out_gain], axis=-1)
    return jnp.einsum('bse,ed->bsd', merged, w_out), k, v


def moe_ffn(h, w_router, b_router, w_gate, b_gate, w_up, b_up, w_down, b_down):
    T, D = h.shape
    logits = (h @ w_router).astype(jnp.float32) + b_router.astype(jnp.float32)
    top_val, top_idx = lax.top_k(logits, TOP_K)
    gates = jax.nn.softmax(top_val, axis=-1).astype(h.dtype)
    n_assign = T * TOP_K
    flat_e = top_idx.reshape(-1).astype(jnp.int32)
    flat_tok = jnp.arange(n_assign, dtype=jnp.int32) // TOP_K
    flat_gate = gates.reshape(-1)
    order = jnp.argsort(flat_e)
    sorted_e = flat_e[order]
    counts = jnp.bincount(flat_e, length=N_EXPERTS).astype(jnp.int32)
    padded = ((counts + MOE_BLOCK - 1) // MOE_BLOCK) * MOE_BLOCK
    start = jnp.cumsum(counts) - counts
    pad_end = jnp.cumsum(padded)
    pad_start = pad_end - padded
    dest = pad_start[sorted_e] + jnp.arange(n_assign, dtype=jnp.int32) - start[sorted_e]
    n_blocks = (n_assign + N_EXPERTS * (MOE_BLOCK - 1) + MOE_BLOCK - 1) // MOE_BLOCK
    n_pad = n_blocks * MOE_BLOCK
    tok_buf = jnp.full((n_pad,), T, jnp.int32).at[dest].set(flat_tok[order])
    gate_buf = jnp.zeros((n_pad,), h.dtype).at[dest].set(flat_gate[order])
    block_start = jnp.arange(n_blocks, dtype=jnp.int32) * MOE_BLOCK
    block_e = jnp.minimum(jnp.sum((pad_end[None, :] <= block_start[:, None]).astype(jnp.int32), axis=1), N_EXPERTS - 1)
    h_pad = jnp.concatenate([h, jnp.zeros((1, D), h.dtype)], axis=0)
    xb = h_pad[tok_buf].reshape(n_blocks, MOE_BLOCK, D)

    def expert_block(args):
        x_blk, e = args
        gt = x_blk @ w_gate[e] + b_gate[e]
        up = x_blk @ w_up[e] + b_up[e]
        gt = jnp.minimum(gt, SWIGLU_LIMIT)
        up = jnp.clip(up, -SWIGLU_LIMIT, SWIGLU_LIMIT)
        act = (up + 1.0) * (gt * jax.nn.sigmoid(SWIGLU_ALPHA * gt))
        return act @ w_down[e] + b_down[e]

    yb = lax.map(expert_block, (xb, block_e))
    y = yb.reshape(n_pad, D) * gate_buf[:, None]
    return jax.ops.segment_sum(y, tok_buf, num_segments=T + 1)[:T]


def setup_inputs(seed: int = 0) -> dict:
    key = jax.random.key(seed)
    ks = jax.random.split(key, 28)

    def nrm(k, shape, s):
        return jax.random.normal(k, shape, jnp.float32) * s

    def gain(k, shape):
        return 1.0 + nrm(k, shape, 0.02)

    return {
        'x_prompt': nrm(ks[0], (BATCH, SEQ, D_MODEL), 1.0),
        'x_sample': nrm(ks[1], (DEC_BATCH, DEC_SEQ, D_MODEL), 1.0),
        'cache_k': nrm(ks[2], (DEC_BATCH, DEPTH, PAST_LEN, KV_HEADS, HEAD_DIM), 1.0),
        'cache_v': nrm(ks[3], (DEC_BATCH, DEPTH, PAST_LEN, KV_HEADS, HEAD_DIM), 1.0),
        'c': nrm(ks[4], (DEC_BATCH, D_MODEL), 1.0),
        'c_ctx': nrm(ks[5], (D_MODEL,), 1.0),
        'w_mod': nrm(ks[6], (DEPTH, D_MODEL, N_MOD * D_MODEL), 0.5 * D_MODEL ** -0.5),
        'b_mod': nrm(ks[7], (DEPTH, N_MOD * D_MODEL), 0.01),
        'norm1': gain(ks[8], (DEPTH, D_MODEL)),
        'w_in': nrm(ks[9], (DEPTH, D_MODEL, IN_W), D_MODEL ** -0.5),
        'q_gain': gain(ks[10], (DEPTH, HEAD_DIM)),
        'k_gain': gain(ks[11], (DEPTH, HEAD_DIM)),
        'w_sp': nrm(ks[12], (DEPTH, G_HEADS, CHUNK, CHUNK), CHUNK ** -0.5),
        'b_sp': gain(ks[13], (DEPTH, G_HEADS, CHUNK)),
        'attn_out_gain': gain(ks[14], (DEPTH, ATTN_W)),
        'gmlp_out_gain': gain(ks[15], (DEPTH, GMLP_W)),
        'w_out': nrm(ks[16], (DEPTH, D_MIX, D_MODEL), D_MIX ** -0.5),
        'norm2': gain(ks[17], (DEPTH, D_MODEL)),
        'w_router': nrm(ks[18], (DEPTH, D_MODEL, N_EXPERTS), D_MODEL ** -0.5),
        'b_router': nrm(ks[19], (DEPTH, N_EXPERTS), 0.01),
        'w_gate': nrm(ks[20], (DEPTH, N_EXPERTS, D_MODEL, D_FF), D_MODEL ** -0.5),
        'b_gate': nrm(ks[21], (DEPTH, N_EXPERTS, D_FF), 0.01),
        'w_up': nrm(ks[22], (DEPTH, N_EXPERTS, D_MODEL, D_FF), D_MODEL ** -0.5),
        'b_up': nrm(ks[23], (DEPTH, N_EXPERTS, D_FF), 0.01),
        'w_down': nrm(ks[24], (DEPTH, N_EXPERTS, D_FF, D_MODEL), D_FF ** -0.5),
        'b_down': nrm(ks[25], (DEPTH, N_EXPERTS, D_MODEL), 0.01),
        'norm_f': gain(ks[26], (D_MODEL,)),
    }


def reference(x_prompt, x_sample, cache_k, cache_v, c, c_ctx, w_mod, b_mod, norm1, w_in, q_gain, k_gain,
              w_sp, b_sp, attn_out_gain, gmlp_out_gain, w_out, norm2, w_router, b_router,
              w_gate, b_gate, w_up, b_up, w_down, b_down, norm_f):
    xp, xs = x_prompt, x_sample
    Bp, Sp = xp.shape[0], xp.shape[1]
    tables = axial_rope_tables(xs.shape[1])
    new_k, new_v = [], []
    for l in range(DEPTH):
        sh1p, sc1p, g1p, sh2p, sc2p, g2p = modulation(c_ctx[None, :], w_mod[l], b_mod[l])
        sh1s, sc1s, g1s, sh2s, sc2s, g2s = modulation(c, w_mod[l], b_mod[l])
        mix_w = (w_in[l], q_gain[l], k_gain[l], w_sp[l], b_sp[l], attn_out_gain[l], gmlp_out_gain[l], w_out[l])
        hp = modulate(xp, norm1[l], sh1p, sc1p)
        op, kp, vp = mixer(hp, None, None, None, *mix_w)
        hs = modulate(xs, norm1[l], sh1s, sc1s)
        os_, _, _ = mixer(hs, cache_k[:, l], cache_v[:, l], tables, *mix_w)
        xp = xp + g1p * op
        xs = xs + g1s * os_
        new_k.append(kp)
        new_v.append(vp)
        hp2 = modulate(xp, norm2[l], sh2p, sc2p).reshape(-1, D_MODEL)
        hs2 = modulate(xs, norm2[l], sh2s, sc2s).reshape(-1, D_MODEL)
        f = moe_ffn(jnp.concatenate([hp2, hs2], axis=0), w_router[l], b_router[l], w_gate[l], b_gate[l],
                    w_up[l], b_up[l], w_down[l], b_down[l])
        n_p = Bp * Sp
        xp = xp + g2p * f[:n_p].reshape(xp.shape)
        xs = xs + g2s * f[n_p:].reshape(xs.shape)
    y_prompt = rms_normalize(xp) * norm_f
    y_sample = rms_normalize(xs) * norm_f
    new_cache_k = jnp.stack(new_k, axis=1)
    new_cache_v = jnp.stack(new_v, axis=1)
    return (y_prompt, y_sample, new_cache_k, new_cache_v)
```

```python
import functools

import jax
import jax.numpy as jnp
from jax import lax
from jax.experimental import pallas as pl
from jax.experimental.pallas import tpu as pltpu

F32 = jnp.float32
BF16 = jnp.bfloat16

D_MODEL = 2048
GRID_W = 64
N_HEADS = 8
KV_HEADS = 2
HEAD_DIM = 128
GQA_GROUP = N_HEADS // KV_HEADS
ATTN_W = N_HEADS * HEAD_DIM
KV_W = KV_HEADS * HEAD_DIM
G_HEADS = 8
G_DIM = (D_MODEL - ATTN_W) // G_HEADS
GMLP_W = G_HEADS * G_DIM
IN_W = ATTN_W + 2 * KV_W + 2 * GMLP_W
CHUNK = 128
ROPE_THETA = 10000.0
AXIS_DIM = HEAD_DIM // 2
N_EXPERTS = 32
TOP_K = 4
D_FF = D_MODEL
SWIGLU_ALPHA = 1.702
SWIGLU_LIMIT = 7.0
N_MOD = 6
EPS = 1e-6

LANES = 128
ROW_TILES = D_MODEL // LANES
MIB = 1024 * 1024

TM = 256
TQ = 128
TC = 128
MOD_TN = 1024
SB = 256
RB = 1792
TF = 256
NF = D_FF // TF
NEG = -1e30


def _rms(x):
    return x * lax.rsqrt(jnp.mean(x * x, axis=-1, keepdims=True) + EPS)


def _mod_kernel(c_ref, w_ref, b_ref, o_ref):
    c = c_ref[...]
    a = c * jax.nn.sigmoid(c)
    o_ref[...] = jnp.dot(a, w_ref[...], preferred_element_type=F32,
                         precision=lax.Precision.HIGHEST) + b_ref[...]


def _modulation(cvec8, w_mod, b_mod):
    n = w_mod.shape[1]
    return pl.pallas_call(
        _mod_kernel,
        out_shape=jax.ShapeDtypeStruct((8, n), F32),
        grid=(n // MOD_TN,),
        in_specs=[pl.BlockSpec((8, D_MODEL), lambda j: (0, 0)),
                  pl.BlockSpec((D_MODEL, MOD_TN), lambda j: (0, j)),
                  pl.BlockSpec((1, MOD_TN), lambda j: (0, j))],
        out_specs=pl.BlockSpec((8, MOD_TN), lambda j: (0, j)),
        compiler_params=pltpu.CompilerParams(dimension_semantics=("arbitrary",),
                                             vmem_limit_bytes=40 * MIB),
        name="modulation",
    )(cvec8, w_mod, b_mod.reshape(1, n))


def _mod_spec(part, row_fn):
    return pl.BlockSpec((None, None, 1, D_MODEL), lambda i: (row_fn(i), part, 0, 0))


def _inproj_kernel(*refs, rope, emit_kv):
    it = iter(refs)
    x_ref, shift_ref, scale_ref, norm_ref, w_ref, qg_ref, kg_ref = (next(it) for _ in range(7))
    cos_ref = sin_ref = None
    if rope:
        cos_ref, sin_ref = next(it), next(it)
    wsp_ref, bsp_ref, gog_ref = next(it), next(it), next(it)
    q_out, k_out, v_out, gm_out = (next(it) for _ in range(4))
    kf_out = vf_out = None
    if emit_kv:
        kf_out, vf_out = next(it), next(it)
    gm_scr = next(it)

    x = x_ref[...]
    h = _rms(x) * (norm_ref[...] * (1.0 + scale_ref[...])) + shift_ref[...]
    z = jnp.dot(h.astype(BF16), w_ref[...], preferred_element_type=F32)

    if rope:
        cosv = cos_ref[...]
        sinv = sin_ref[...]
        lane = lax.broadcasted_iota(jnp.int32, (TM, HEAD_DIM), 1)
        first_half = (lane % AXIS_DIM) < (AXIS_DIM // 2)

    def head(zc, gain):
        n = _rms(zc) * gain
        if rope:
            partner = jnp.where(first_half, pltpu.roll(n, HEAD_DIM - AXIS_DIM // 2, 1),
                                pltpu.roll(n, AXIS_DIM // 2, 1))
            n = n * cosv + partner * sinv
        return n

    qg = qg_ref[...]
    kg = kg_ref[...]
    sm_scale = HEAD_DIM ** -0.5
    for hh in range(N_HEADS):
        sl = slice(hh * HEAD_DIM, (hh + 1) * HEAD_DIM)
        q_out[:, sl] = (head(z[:, sl], qg) * sm_scale).astype(BF16)
    for hh in range(KV_HEADS):
        sl = slice(hh * HEAD_DIM, (hh + 1) * HEAD_DIM)
        kh = head(z[:, ATTN_W + hh * HEAD_DIM:ATTN_W + (hh + 1) * HEAD_DIM], kg)
        k_out[:, sl] = kh.astype(BF16)
        if emit_kv:
            kf_out[:, sl] = kh
    v = z[:, ATTN_W + KV_W:ATTN_W + 2 * KV_W]
    v_out[...] = v.astype(BF16)
    if emit_kv:
        vf_out[...] = v

    u0 = ATTN_W + 2 * KV_W
    g0 = u0 + GMLP_W
    for hh in range(G_HEADS):
        sl = slice(hh * G_DIM, (hh + 1) * G_DIM)
        u = jax.nn.gelu(z[:, u0 + hh * G_DIM:u0 + (hh + 1) * G_DIM])
        g = _rms(jax.nn.gelu(z[:, g0 + hh * G_DIM:g0 + (hh + 1) * G_DIM])).astype(BF16)
        w_h = wsp_ref[hh]
        b_h = bsp_ref[hh]
        for c in range(TM // CHUNK):
            rs = slice(c * CHUNK, (c + 1) * CHUNK)
            sp = jnp.dot(w_h, g[rs], preferred_element_type=F32) + b_h
            gm_scr[rs, sl] = u[rs] * sp
    gm_out[...] = (_rms(gm_scr[...]) * gog_ref[...]).astype(BF16)


def _inproj(x2d, mod4, row_fn, norm1, w_in_bf, q_gain, k_gain, rope_tabs, wsp_bf, bsp_b, gog,
            seq_len, emit_kv):
    t = x2d.shape[0]
    rope = rope_tabs is not None
    tiles_per_seq = seq_len // TM
    const = lambda i: (0, 0)
    in_specs = [pl.BlockSpec((TM, D_MODEL), lambda i: (i, 0)),
                _mod_spec(0, row_fn), _mod_spec(1, row_fn),
                pl.BlockSpec((1, D_MODEL), const),
                pl.BlockSpec((D_MODEL, IN_W), const, pipeline_mode=pl.Buffered(1)),
                pl.BlockSpec((1, HEAD_DIM), const),
                pl.BlockSpec((1, HEAD_DIM), const)]
    args = [x2d, mod4, mod4, norm1, w_in_bf, q_gain, k_gain]
    if rope:
        tab_spec = pl.BlockSpec((TM, HEAD_DIM), lambda i: (i % tiles_per_seq, 0))
        in_specs += [tab_spec, tab_spec]
        args += list(rope_tabs)
    in_specs += [pl.BlockSpec((G_HEADS, CHUNK, CHUNK), lambda i: (0, 0, 0)),
                 pl.BlockSpec((G_HEADS, CHUNK, G_DIM), lambda i: (0, 0, 0)),
                 pl.BlockSpec((1, GMLP_W), const)]
    args += [wsp_bf, bsp_b, gog]
    row = lambda w: pl.BlockSpec((TM, w), lambda i: (i, 0))
    out_shape = [jax.ShapeDtypeStruct((t, ATTN_W), BF16), jax.ShapeDtypeStruct((t, KV_W), BF16),
                 jax.ShapeDtypeStruct((t, KV_W), BF16), jax.ShapeDtypeStruct((t, GMLP_W), BF16)]
    out_specs = [row(ATTN_W), row(KV_W), row(KV_W), row(GMLP_W)]
    if emit_kv:
        out_shape += [jax.ShapeDtypeStruct((t, KV_W), F32)] * 2
        out_specs += [row(KV_W), row(KV_W)]
    return pl.pallas_call(
        functools.partial(_inproj_kernel, rope=rope, emit_kv=emit_kv),
        out_shape=out_shape,
        grid=(t // TM,),
        in_specs=in_specs,
        out_specs=out_specs,
        scratch_shapes=[pltpu.VMEM((TM, GMLP_W), F32)],
        compiler_params=pltpu.CompilerParams(dimension_semantics=("arbitrary",),
                                             vmem_limit_bytes=48 * MIB),
        name="inproj_rope" if rope else "inproj",
    )(*args)


def _attn_kernel(q_ref, k_ref, v_ref, gain_ref, o_ref):
    outs = [None] * N_HEADS
    for kvh in range(KV_HEADS):
        ks = slice(kvh * HEAD_DIM, (kvh + 1) * HEAD_DIM)
        kh = k_ref[:, ks]
        vh = v_ref[:, ks]
        q4 = jnp.concatenate(
            [q_ref[:, (kvh * GQA_GROUP + g) * HEAD_DIM:(kvh * GQA_GROUP + g + 1) * HEAD_DIM]
             for g in range(GQA_GROUP)], axis=0)
        s = lax.dot_general(q4, kh, (((1,), (1,)), ((), ())), preferred_element_type=F32)
        m = jnp.max(s, axis=-1, keepdims=True)
        p = jnp.exp(s - m)
        l = jnp.sum(p, axis=-1, keepdims=True)
        o = jnp.dot(p.astype(BF16), vh, preferred_element_type=F32) / l
        for g in range(GQA_GROUP):
            outs[kvh * GQA_GROUP + g] = o[g * TQ:(g + 1) * TQ]
    ss = outs[0] * outs[0]
    for hh in range(1, N_HEADS):
        ss = ss + outs[hh] * outs[hh]
    inv = lax.rsqrt(jnp.sum(ss, axis=-1, keepdims=True) * (1.0 / ATTN_W) + EPS)
    gain = gain_ref[...]
    for hh in range(N_HEADS):
        sl = slice(hh * HEAD_DIM, (hh + 1) * HEAD_DIM)
        o_ref[:, sl] = (outs[hh] * inv * gain[:, sl]).astype(BF16)


def _attention(q2d, k3d, v3d, gain, seq_len):
    t = q2d.shape[0]
    b, l, _ = k3d.shape
    nq = seq_len // TQ
    return pl.pallas_call(
        _attn_kernel,
        out_shape=jax.ShapeDtypeStruct((t, ATTN_W), BF16),
        grid=(b, nq),
        in_specs=[pl.BlockSpec((TQ, ATTN_W), lambda bi, i: (bi * nq + i, 0)),
                  pl.BlockSpec((None, l, KV_W), lambda bi, i: (bi, 0, 0)),
                  pl.BlockSpec((None, l, KV_W), lambda bi, i: (bi, 0, 0)),
                  pl.BlockSpec((1, ATTN_W), lambda bi, i: (0, 0))],
        out_specs=pl.BlockSpec((TQ, ATTN_W), lambda bi, i: (bi * nq + i, 0)),
        compiler_params=pltpu.CompilerParams(dimension_semantics=("arbitrary", "arbitrary"),
                                             vmem_limit_bytes=48 * MIB),
        name="attention",
    )(q2d, k3d, v3d, gain)


def _outproj_kernel(a_ref, gm_ref, w_ref, x_ref, g1_ref, n2_ref, sc2_ref, sh2_ref, wr_ref, br_ref,
                    cnt_in_ref, x1_out, h2_out, idx_out, gate_out, rank_out, cnt_out, cnt_scr):
    i = pl.program_id(0)

    @pl.when(i == 0)
    def _():
        cnt_scr[...] = cnt_in_ref[...]

    o = (jnp.dot(a_ref[...], w_ref[0:ATTN_W, :], preferred_element_type=F32)
         + jnp.dot(gm_ref[...], w_ref[ATTN_W:, :], preferred_element_type=F32))
    x1 = x_ref[...] + g1_ref[...] * o
    x1_out[...] = x1
    h2 = _rms(x1) * (n2_ref[...] * (1.0 + sc2_ref[...])) + sh2_ref[...]
    for s in range(ROW_TILES):
        h2_out[pl.ds(s, TM, stride=ROW_TILES), :] = h2[:, s * LANES:(s + 1) * LANES]

    h_hi = h2.astype(BF16)
    h_lo = (h2 - h_hi.astype(F32)).astype(BF16)
    wr = wr_ref[...]
    w_hi = wr.astype(BF16)
    w_lo = (wr - w_hi.astype(F32)).astype(BF16)
    logits = (jnp.dot(h_hi, w_hi, preferred_element_type=F32)
              + jnp.dot(h_lo, w_hi, preferred_element_type=F32)
              + jnp.dot(h_hi, w_lo, preferred_element_type=F32)) + br_ref[...]

    lane = lax.broadcasted_iota(jnp.int32, (TM, LANES), 1)
    lane_f = lane.astype(F32)
    work = logits
    vals, sels = [], []
    multi = jnp.zeros((TM, LANES), F32)
    idx_acc = jnp.zeros((TM, LANES), F32)
    for k in range(TOP_K):
        m = jnp.max(work, axis=-1, keepdims=True)
        idx = jnp.min(jnp.where(work == m, lane_f, float(LANES)), axis=-1, keepdims=True)
        sel = lane_f == idx
        vals.append(m)
        sels.append(sel)
        work = jnp.where(sel, -jnp.inf, work)
        multi = multi + sel.astype(F32)
        idx_acc = jnp.where(lane == k, idx, idx_acc)
    es = [jnp.exp(v - vals[0]) for v in vals]
    den = es[0] + es[1] + es[2] + es[3]
    gate_acc = jnp.zeros((TM, LANES), F32)
    for k in range(TOP_K):
        gate_acc = jnp.where(lane == k, es[k] / den, gate_acc)

    r_i = lax.broadcasted_iota(jnp.int32, (TM, TM), 0)
    c_i = lax.broadcasted_iota(jnp.int32, (TM, TM), 1)
    lower = (c_i < r_i).astype(BF16)
    prefix = jnp.dot(lower, multi.astype(BF16), preferred_element_type=F32) + cnt_scr[...]
    rank_acc = jnp.zeros((TM, LANES), F32)
    for k in range(TOP_K):
        r = jnp.sum(jnp.where(sels[k], prefix, 0.0), axis=-1, keepdims=True)
        rank_acc = jnp.where(lane == k, r, rank_acc)
    cnt_new = cnt_scr[...] + jnp.sum(multi, axis=0, keepdims=True)
    cnt_scr[...] = cnt_new
    cnt_out[...] = cnt_new
    idx_out[...] = idx_acc.astype(jnp.int32)
    gate_out[...] = gate_acc
    rank_out[...] = rank_acc.astype(jnp.int32)


def _outproj(attn_n, gm_n, w_out_bf, x2d, mod4, row_fn, norm2, wr_pad, br_pad, cnt_in):
    t = x2d.shape[0]
    const = lambda i: (0, 0)
    row = lambda w: pl.BlockSpec((TM, w), lambda i: (i, 0))
    return pl.pallas_call(
        _outproj_kernel,
        out_shape=[jax.ShapeDtypeStruct((t, D_MODEL), F32),
                   jax.ShapeDtypeStruct((t * ROW_TILES, LANES), F32),
                   jax.ShapeDtypeStruct((t, LANES), jnp.int32),
                   jax.ShapeDtypeStruct((t, LANES), F32),
                   jax.ShapeDtypeStruct((t, LANES), jnp.int32),
                   jax.ShapeDtypeStruct((1, LANES), F32)],
        grid=(t // TM,),
        in_specs=[row(ATTN_W), row(GMLP_W),
                  pl.BlockSpec((D_MODEL, D_MODEL), const, pipeline_mode=pl.Buffered(1)),
                  row(D_MODEL),
                  _mod_spec(2, row_fn),
                  pl.BlockSpec((1, D_MODEL), const),
                  _mod_spec(4, row_fn), _mod_spec(3, row_fn),
                  pl.BlockSpec((D_MODEL, LANES), const),
                  pl.BlockSpec((1, LANES), const),
                  pl.BlockSpec((1, LANES), const)],
        out_specs=[row(D_MODEL),
                   pl.BlockSpec((TM * ROW_TILES, LANES), lambda i: (i, 0)),
                   row(LANES), row(LANES), row(LANES),
                   pl.BlockSpec((1, LANES), const)],
        scratch_shapes=[pltpu.VMEM((1, LANES), F32)],
        compiler_params=pltpu.CompilerParams(dimension_semantics=("arbitrary",),
                                             vmem_limit_bytes=48 * MIB),
        name="outproj_router",
    )(attn_n, gm_n, w_out_bf, x2d, mod4, norm2, mod4, mod4, wr_pad, br_pad, cnt_in)


def _moe_kernel(ie_ref, irow_ref, insb_ref, inv_ref, tok_ref, aid_ref,
                h2_hbm, wg_hbm, wu_hbm, wd_hbm, bg_ref, bu_ref, bd_ref,
                y_hbm,
                xg, ybuf, wgs, wus, wds, wgb, wub, wdb, ystage, sem_x, sem_w, sem_y):
    it = pl.program_id(0)
    nsb = insb_ref[it]

    def x_copy(tok, r):
        return pltpu.make_async_copy(
            h2_hbm.at[pl.ds(pl.multiple_of(tok * ROW_TILES, ROW_TILES), ROW_TILES)],
            xg.at[pl.ds(pl.multiple_of(r * ROW_TILES, ROW_TILES), ROW_TILES)], sem_x)

    def w_copies(e, f, slot):
        cols = pl.ds(f * TF, TF)
        return (pltpu.make_async_copy(wg_hbm.at[e, :, cols], wgs.at[slot], sem_w.at[slot, 0]),
                pltpu.make_async_copy(wu_hbm.at[e, :, cols], wus.at[slot], sem_w.at[slot, 1]),
                pltpu.make_async_copy(wd_hbm.at[e, cols, :], wds.at[slot], sem_w.at[slot, 2]))

    def y_copy(slot, j, aid):
        return pltpu.make_async_copy(
            ystage.at[slot, pl.ds(pl.multiple_of(j * ROW_TILES, ROW_TILES), ROW_TILES)],
            y_hbm.at[pl.ds(pl.multiple_of(aid * ROW_TILES, ROW_TILES), ROW_TILES)], sem_y.at[slot])

    def y_wait(slot, n_rows=SB):
        n = n_rows * ROW_TILES
        pltpu.make_async_copy(ystage.at[slot, pl.ds(0, n)], y_hbm.at[pl.ds(0, n)], sem_y.at[slot]).wait()

    @pl.when(nsb > 0)
    def _():
        e = ie_ref[it]
        row0 = irow_ref[it]
        n_valid = inv_ref[it]

        def issue_x(r, carry):
            x_copy(tok_ref[row0 + r], r).start()
            return carry
        lax.fori_loop(0, nsb * SB, issue_x, 0)
        for cp in w_copies(e, 0, 0):
            cp.start()

        def wait_x(sb, carry):
            pltpu.make_async_copy(h2_hbm.at[pl.ds(0, SB * ROW_TILES)],
                                  xg.at[pl.ds(pl.multiple_of(sb * (SB * ROW_TILES), SB * ROW_TILES),
                                              SB * ROW_TILES)], sem_x).wait()
            return carry
        lax.fori_loop(0, nsb, wait_x, 0)

        for f in range(NF):
            slot = f % 2
            if f + 1 < NF:
                for cp in w_copies(e, f + 1, 1 - slot):
                    cp.start()
            for cp in w_copies(e, f, slot):
                cp.wait()
            wgb[...] = wgs[slot].astype(BF16)
            wub[...] = wus[slot].astype(BF16)
            wdb[...] = wds[slot].astype(BF16)
            fs = slice(f * TF, (f + 1) * TF)
            b_g = bg_ref[pl.ds(e, 1), fs]
            b_u = bu_ref[pl.ds(e, 1), fs]

            def sb_body(sb, carry, f=f, b_g=b_g, b_u=b_u):
                base = sb * (SB * ROW_TILES)
                x = jnp.concatenate(
                    [xg[pl.ds(base + s, SB, stride=ROW_TILES), :] for s in range(ROW_TILES)],
                    axis=1).astype(BF16)
                gt = jnp.dot(x, wgb[...], preferred_element_type=F32) + b_g
                up = jnp.dot(x, wub[...], preferred_element_type=F32) + b_u
                gt = jnp.minimum(gt, SWIGLU_LIMIT)
                up = jnp.clip(up, -SWIGLU_LIMIT, SWIGLU_LIMIT)
                act = (up + 1.0) * (gt * jax.nn.sigmoid(SWIGLU_ALPHA * gt))
                contrib = jnp.dot(act.astype(BF16), wdb[...], preferred_element_type=F32)
                rows = pl.ds(pl.multiple_of(sb * SB, SB), SB)
                if f == 0:
                    ybuf[rows, :] = contrib + bd_ref[pl.ds(e, 1), :]
                elif f < NF - 1:
                    ybuf[rows, :] += contrib
                else:
                    y = ybuf[rows, :] + contrib
                    yslot = sb % 2

                    @pl.when(sb >= 2)
                    def _():
                        y_wait(yslot)
                    ys = ystage.at[yslot]
                    for s in range(ROW_TILES):
                        ys[pl.ds(s, SB, stride=ROW_TILES), :] = y[:, s * LANES:(s + 1) * LANES]

                    def issue_y(j, c2):
                        y_copy(yslot, j, aid_ref[row0 + sb * SB + j]).start()
                        return c2
                    lax.fori_loop(0, jnp.minimum(n_valid - sb * SB, SB), issue_y, 0)
                return carry
            lax.fori_loop(0, nsb, sb_body, 0)

        y_wait((nsb - 1) % 2, n_valid - (nsb - 1) * SB)

        @pl.when(nsb >= 2)
        def _():
            y_wait(nsb % 2)


def _moe(item_e, item_row, item_nsb, item_nvalid, tok_sorted, aid_sorted, h2g, w_gate, w_up, w_down,
         b_gate, b_up, b_down, n_items, n_out_rows):
    any_spec = pl.BlockSpec(memory_space=pl.ANY)
    bias_spec = pl.BlockSpec((N_EXPERTS, D_MODEL), lambda i, *_: (0, 0))
    grid_spec = pltpu.PrefetchScalarGridSpec(
        num_scalar_prefetch=6,
        grid=(n_items,),
        in_specs=[any_spec, any_spec, any_spec, any_spec, bias_spec, bias_spec, bias_spec],
        out_specs=any_spec,
        scratch_shapes=[pltpu.VMEM((RB * ROW_TILES, LANES), F32),
                        pltpu.VMEM((RB, D_MODEL), F32),
                        pltpu.VMEM((2, D_MODEL, TF), F32),
                        pltpu.VMEM((2, D_MODEL, TF), F32),
                        pltpu.VMEM((2, TF, D_MODEL), F32),
                        pltpu.VMEM((D_MODEL, TF), BF16),
                        pltpu.VMEM((D_MODEL, TF), BF16),
                        pltpu.VMEM((TF, D_MODEL), BF16),
                        pltpu.VMEM((2, SB * ROW_TILES, LANES), F32),
                        pltpu.SemaphoreType.DMA,
                        pltpu.SemaphoreType.DMA((2, 3)),
                        pltpu.SemaphoreType.DMA((2,))],
    )
    return pl.pallas_call(
        _moe_kernel,
        out_shape=jax.ShapeDtypeStruct((n_out_rows * ROW_TILES, LANES), F32),
        grid_spec=grid_spec,
        compiler_params=pltpu.CompilerParams(dimension_semantics=("arbitrary",),
                                             vmem_limit_bytes=58 * MIB),
        name="moe_experts",
    )(item_e, item_row, item_nsb, item_nvalid, tok_sorted, aid_sorted, h2g, w_gate, w_up, w_down,
      b_gate, b_up, b_down)


def _combine_kernel(y0, y1, y2, y3, gate_ref, x1_ref, g2_ref, nf_ref, o_ref, f_scr):
    gate = gate_ref[...]
    for k, yk in enumerate((y0, y1, y2, y3)):
        gk = gate[:, k:k + 1]
        for s in range(ROW_TILES):
            sl = slice(s * LANES, (s + 1) * LANES)
            part = gk * yk[pl.ds(s, TC, stride=ROW_TILES), :]
            if k == 0:
                f_scr[:, sl] = part
            else:
                f_scr[:, sl] += part
    x2 = x1_ref[...] + g2_ref[...] * f_scr[...]
    o_ref[...] = _rms(x2) * nf_ref[...]


def _combine(ytok, gates, x1, mod4, row_fn, norm_f, tok_off, t_total):
    t = x1.shape[0]
    blocks_total = t_total // TC
    off = tok_off // TC

    def y_spec(k):
        return pl.BlockSpec((TC * ROW_TILES, LANES), lambda i: (k * blocks_total + off + i, 0))
    row = lambda w: pl.BlockSpec((TC, w), lambda i: (i, 0))
    return pl.pallas_call(
        _combine_kernel,
        out_shape=jax.ShapeDtypeStruct((t, D_MODEL), F32),
        grid=(t // TC,),
        in_specs=[y_spec(0), y_spec(1), y_spec(2), y_spec(3), row(LANES), row(D_MODEL),
                  pl.BlockSpec((None, None, 1, D_MODEL),
                               lambda i: (row_fn(i * TC // TM), 5, 0, 0)),
                  pl.BlockSpec((1, D_MODEL), lambda i: (0, 0))],
        out_specs=row(D_MODEL),
        scratch_shapes=[pltpu.VMEM((TC, D_MODEL), F32)],
        compiler_params=pltpu.CompilerParams(dimension_semantics=("arbitrary",),
                                             vmem_limit_bytes=40 * MIB),
        name="combine_norm",
    )(ytok, ytok, ytok, ytok, gates, x1, mod4, norm_f)


def _rope_tables(n_tokens):
    n_rows = n_tokens // GRID_W
    row = jnp.repeat(jnp.arange(n_rows, dtype=F32), GRID_W)
    col = jnp.tile(jnp.arange(GRID_W, dtype=F32), n_rows)
    half = AXIS_DIM // 2
    inv_freq = ROPE_THETA ** (-jnp.arange(half, dtype=F32) / half)
    ang_r = row[:, None] * inv_freq[None, :]
    ang_c = col[:, None] * inv_freq[None, :]
    cos_t = jnp.concatenate([jnp.cos(ang_r)] * 2 + [jnp.cos(ang_c)] * 2, axis=-1)
    sin_t = jnp.concatenate([-jnp.sin(ang_r), jnp.sin(ang_r), -jnp.sin(ang_c), jnp.sin(ang_c)], axis=-1)
    return cos_t, sin_t


def kernel(x_prompt, x_sample, cache_k, cache_v, c, c_ctx, w_mod, b_mod, norm1, w_in, q_gain, k_gain,
           w_sp, b_sp, attn_out_gain, gmlp_out_gain, w_out, norm2, w_router, b_router,
           w_gate, b_gate, w_up, b_up, w_down, b_down, norm_f):
    bp, sp_len, _ = x_prompt.shape
    bs, ss_len, _ = x_sample.shape
    depth = w_mod.shape[0]
    n_p = bp * sp_len
    n_s = bs * ss_len
    t_total = n_p + n_s
    n_assign = t_total * TOP_K
    n_sub = (n_assign + N_EXPERTS * (SB - 1)) // SB + 1
    n_pad = n_sub * SB
    n_items = N_EXPERTS + -(-n_pad // RB)

    xp = x_prompt.reshape(n_p, D_MODEL)
    xs = x_sample.reshape(n_s, D_MODEL)
    tiles_per_sample = ss_len // TM
    row_p = lambda i: 0
    row_s = lambda i: 1 + i // tiles_per_sample
    rope_tabs = _rope_tables(ss_len)
    cvec8 = jnp.zeros((8, D_MODEL), F32).at[0].set(c_ctx).at[1:1 + bs].set(c)
    new_k, new_v = [], []

    for l in range(depth):
        mod4 = _modulation(cvec8, w_mod[l], b_mod[l]).reshape(8, N_MOD, 1, D_MODEL)
        w_in_bf = w_in[l].astype(BF16)
        w_out_bf = w_out[l].astype(BF16)
        wsp_bf = w_sp[l].astype(BF16)
        bsp_b = jnp.broadcast_to(b_sp[l][:, :, None], (G_HEADS, CHUNK, G_DIM))
        n1 = norm1[l].reshape(1, D_MODEL)
        n2 = norm2[l].reshape(1, D_MODEL)
        qg = q_gain[l].reshape(1, HEAD_DIM)
        kg = k_gain[l].reshape(1, HEAD_DIM)
        gog = gmlp_out_gain[l].reshape(1, GMLP_W)
        aog = attn_out_gain[l].reshape(1, ATTN_W)
        wr_pad = jnp.zeros((D_MODEL, LANES), F32).at[:, :N_EXPERTS].set(w_router[l])
        br_pad = jnp.full((1, LANES), NEG, F32).at[0, :N_EXPERTS].set(b_router[l])

        qp, kp, vp, gmp, kpf, vpf = _inproj(xp, mod4, row_p, n1, w_in_bf, qg, kg, None, wsp_bf, bsp_b,
                                            gog, sp_len, True)
        qs, ks, vs, gms = _inproj(xs, mod4, row_s, n1, w_in_bf, qg, kg, rope_tabs, wsp_bf, bsp_b,
                                  gog, ss_len, False)
        new_k.append(kpf.reshape(bp, sp_len, KV_HEADS, HEAD_DIM))
        new_v.append(vpf.reshape(bp, sp_len, KV_HEADS, HEAD_DIM))

        attn_p = _attention(qp, kp.reshape(bp, sp_len, KV_W), vp.reshape(bp, sp_len, KV_W), aog, sp_len)
        k_all = jnp.concatenate([cache_k[:, l].reshape(bs, -1, KV_W).astype(BF16),
                                 ks.reshape(bs, ss_len, KV_W)], axis=1)
        v_all = jnp.concatenate([cache_v[:, l].reshape(bs, -1, KV_W).astype(BF16),
                                 vs.reshape(bs, ss_len, KV_W)], axis=1)
        attn_s = _attention(qs, k_all, v_all, aog, ss_len)

        cnt0 = jnp.zeros((1, LANES), F32)
        x1p, h2p, idxp, gatep, rankp, cnt1 = _outproj(attn_p, gmp, w_out_bf, xp, mod4, row_p, n2,
                                                      wr_pad, br_pad, cnt0)
        x1s, h2s, idxs, gates_s, ranks, cnt2 = _outproj(attn_s, gms, w_out_bf, xs, mod4, row_s, n2,
                                                        wr_pad, br_pad, cnt1)

        idx = jnp.concatenate([idxp[:, :TOP_K], idxs[:, :TOP_K]], axis=0)
        rank = jnp.concatenate([rankp[:, :TOP_K], ranks[:, :TOP_K]], axis=0)
        counts = cnt2[0, :N_EXPERTS].astype(jnp.int32)
        padded = ((counts + SB - 1) // SB) * SB
        pad_end = jnp.cumsum(padded)
        pad_start = pad_end - padded
        dest = pad_start[idx] + rank
        slot_id = (jnp.arange(TOP_K, dtype=jnp.int32)[None, :] * t_total
                   + jnp.arange(t_total, dtype=jnp.int32)[:, None])
        aid_sorted = jnp.full((n_pad,), -1, jnp.int32).at[dest.reshape(-1)].set(slot_id.reshape(-1))
        is_pad = aid_sorted < 0
        tok_sorted = jnp.where(is_pad, 0, aid_sorted % t_total)
        aid_sorted = jnp.where(is_pad, 0, aid_sorted)
        items_e = (padded + RB - 1) // RB
        items_end = jnp.cumsum(items_e)
        j = jnp.arange(n_items, dtype=jnp.int32)
        item_e = jnp.minimum(jnp.sum((items_end[None, :] <= j[:, None]).astype(jnp.int32), axis=1),
                             N_EXPERTS - 1)
        within = j - (items_end - items_e)[item_e]
        item_row = pad_start[item_e] + within * RB
        item_rows = jnp.clip(padded[item_e] - within * RB, 0, RB)
        item_nsb = jnp.where(j < items_end[-1], item_rows // SB, 0).astype(jnp.int32)
        item_row = jnp.where(item_nsb > 0, item_row, 0).astype(jnp.int32)
        item_nvalid = jnp.clip(counts[item_e] - within * RB, 0, item_nsb * SB).astype(jnp.int32)

        h2g = jnp.concatenate([h2p, h2s], axis=0)
        ytok = _moe(item_e.astype(jnp.int32), item_row, item_nsb, item_nvalid, tok_sorted, aid_sorted,
                    h2g, w_gate[l], w_up[l], w_down[l], b_gate[l], b_up[l], b_down[l], n_items,
                    n_assign)

        nf = norm_f.reshape(1, D_MODEL)
        is_last = l == depth - 1
        assert is_last, "only DEPTH == 1 is supported"
        y_prompt = _combine(ytok, gatep, x1p, mod4, row_p, nf, 0, t_total)
        y_sample = _combine(ytok, gates_s, x1s, mod4, row_s, nf, n_p, t_total)

    return (y_prompt.reshape(x_prompt.shape), y_sample.reshape(x_sample.shape),
            jnp.stack(new_k, axis=1), jnp.stack(new_v, axis=1))
```

```python
import functools

import jax
import jax.numpy as jnp
from jax import lax
from jax.experimental import pallas as pl
from jax.experimental.pallas import tpu as pltpu

F32 = jnp.float32
BF16 = jnp.bfloat16

D_MODEL = 2048
GRID_W = 64
N_HEADS = 8
KV_HEADS = 2
HEAD_DIM = 128
GQA_GROUP = N_HEADS // KV_HEADS
ATTN_W = N_HEADS * HEAD_DIM
KV_W = KV_HEADS * HEAD_DIM
G_HEADS = 8
G_DIM = (D_MODEL - ATTN_W) // G_HEADS
GMLP_W = G_HEADS * G_DIM
IN_W = ATTN_W + 2 * KV_W + 2 * GMLP_W
CHUNK = 128
ROPE_THETA = 10000.0
AXIS_DIM = HEAD_DIM // 2
N_EXPERTS = 32
TOP_K = 4
D_FF = D_MODEL
SWIGLU_ALPHA = 1.702
SWIGLU_LIMIT = 7.0
N_MOD = 6
EPS = 1e-6

LANES = 128
ROW_TILES = D_MODEL // LANES
MIB = 1024 * 1024

TM = 256
TQ = 128
TC = 128
MOD_TN = 1024
SB = 256
RB = 1792
TF = 256
NF = D_FF // TF
TN = 512
NN = D_MODEL // TN
GCH = 32
LAG = 4
RING = 6
YCH = SB // NN
NEG = -1e30


def _rms(x):
    return x * lax.rsqrt(jnp.mean(x * x, axis=-1, keepdims=True) + EPS)


def _aligned(x, m):
    if isinstance(x, int):
        assert x % m == 0
        return x
    return pl.multiple_of(x, m)


def _mod_kernel(c_ref, w_ref, b_ref, o_ref):
    c = c_ref[...]
    a = c * jax.nn.sigmoid(c)
    o_ref[...] = jnp.dot(a, w_ref[...], preferred_element_type=F32,
                         precision=lax.Precision.HIGHEST) + b_ref[...]


def _modulation(cvec8, w_mod, b_mod):
    n = w_mod.shape[1]
    return pl.pallas_call(
        _mod_kernel,
        out_shape=jax.ShapeDtypeStruct((8, n), F32),
        grid=(n // MOD_TN,),
        in_specs=[pl.BlockSpec((8, D_MODEL), lambda j: (0, 0)),
                  pl.BlockSpec((D_MODEL, MOD_TN), lambda j: (0, j)),
                  pl.BlockSpec((1, MOD_TN), lambda j: (0, j))],
        out_specs=pl.BlockSpec((8, MOD_TN), lambda j: (0, j)),
        compiler_params=pltpu.CompilerParams(dimension_semantics=("arbitrary",),
                                             vmem_limit_bytes=40 * MIB),
        name="modulation",
    )(cvec8, w_mod, b_mod.reshape(1, n))


def _mod_spec(part, row_fn):
    return pl.BlockSpec((None, None, 1, D_MODEL), lambda i: (row_fn(i), part, 0, 0))


def _inproj_kernel(*refs, rope, emit_kv):
    it = iter(refs)
    x_ref, shift_ref, scale_ref, norm_ref, w_ref, qg_ref, kg_ref = (next(it) for _ in range(7))
    cos_ref = sin_ref = None
    if rope:
        cos_ref, sin_ref = next(it), next(it)
    wsp_ref, bsp_ref, gog_ref = next(it), next(it), next(it)
    q_out, k_out, v_out, gm_out = (next(it) for _ in range(4))
    kf_out = vf_out = None
    if emit_kv:
        kf_out, vf_out = next(it), next(it)
    gm_scr = next(it)

    x = x_ref[...]
    h = _rms(x) * (norm_ref[...] * (1.0 + scale_ref[...])) + shift_ref[...]
    z = jnp.dot(h.astype(BF16), w_ref[...], preferred_element_type=F32)

    if rope:
        cosv = cos_ref[...]
        sinv = sin_ref[...]
        lane = lax.broadcasted_iota(jnp.int32, (TM, HEAD_DIM), 1)
        first_half = (lane % AXIS_DIM) < (AXIS_DIM // 2)

    def head(zc, gain):
        n = _rms(zc) * gain
        if rope:
            partner = jnp.where(first_half, pltpu.roll(n, HEAD_DIM - AXIS_DIM // 2, 1),
                                pltpu.roll(n, AXIS_DIM // 2, 1))
            n = n * cosv + partner * sinv
        return n

    qg = qg_ref[...]
    kg = kg_ref[...]
    sm_scale = HEAD_DIM ** -0.5
    for hh in range(N_HEADS):
        sl = slice(hh * HEAD_DIM, (hh + 1) * HEAD_DIM)
        q_out[:, sl] = (head(z[:, sl], qg) * sm_scale).astype(BF16)
    for hh in range(KV_HEADS):
        sl = slice(hh * HEAD_DIM, (hh + 1) * HEAD_DIM)
        kh = head(z[:, ATTN_W + hh * HEAD_DIM:ATTN_W + (hh + 1) * HEAD_DIM], kg)
        k_out[:, sl] = kh.astype(BF16)
        if emit_kv:
            kf_out[:, sl] = kh
    v = z[:, ATTN_W + KV_W:ATTN_W + 2 * KV_W]
    v_out[...] = v.astype(BF16)
    if emit_kv:
        vf_out[...] = v

    u0 = ATTN_W + 2 * KV_W
    g0 = u0 + GMLP_W
    for hh in range(G_HEADS):
        sl = slice(hh * G_DIM, (hh + 1) * G_DIM)
        u = jax.nn.gelu(z[:, u0 + hh * G_DIM:u0 + (hh + 1) * G_DIM])
        g = _rms(jax.nn.gelu(z[:, g0 + hh * G_DIM:g0 + (hh + 1) * G_DIM])).astype(BF16)
        w_h = wsp_ref[hh]
        b_h = bsp_ref[hh]
        for c in range(TM // CHUNK):
            rs = slice(c * CHUNK, (c + 1) * CHUNK)
            sp = jnp.dot(w_h, g[rs], preferred_element_type=F32) + b_h
            gm_scr[rs, sl] = u[rs] * sp
    gm_out[...] = (_rms(gm_scr[...]) * gog_ref[...]).astype(BF16)


def _inproj(x2d, mod4, row_fn, norm1, w_in_bf, q_gain, k_gain, rope_tabs, wsp_bf, bsp_b, gog,
            seq_len, emit_kv):
    t = x2d.shape[0]
    rope = rope_tabs is not None
    tiles_per_seq = seq_len // TM
    const = lambda i: (0, 0)
    in_specs = [pl.BlockSpec((TM, D_MODEL), lambda i: (i, 0)),
                _mod_spec(0, row_fn), _mod_spec(1, row_fn),
                pl.BlockSpec((1, D_MODEL), const),
                pl.BlockSpec((D_MODEL, IN_W), const, pipeline_mode=pl.Buffered(1)),
                pl.BlockSpec((1, HEAD_DIM), const),
                pl.BlockSpec((1, HEAD_DIM), const)]
    args = [x2d, mod4, mod4, norm1, w_in_bf, q_gain, k_gain]
    if rope:
        tab_spec = pl.BlockSpec((TM, HEAD_DIM), lambda i: (i % tiles_per_seq, 0))
        in_specs += [tab_spec, tab_spec]
        args += list(rope_tabs)
    in_specs += [pl.BlockSpec((G_HEADS, CHUNK, CHUNK), lambda i: (0, 0, 0)),
                 pl.BlockSpec((G_HEADS, CHUNK, G_DIM), lambda i: (0, 0, 0)),
                 pl.BlockSpec((1, GMLP_W), const)]
    args += [wsp_bf, bsp_b, gog]
    row = lambda w: pl.BlockSpec((TM, w), lambda i: (i, 0))
    out_shape = [jax.ShapeDtypeStruct((t, ATTN_W), BF16), jax.ShapeDtypeStruct((t, KV_W), BF16),
                 jax.ShapeDtypeStruct((t, KV_W), BF16), jax.ShapeDtypeStruct((t, GMLP_W), BF16)]
    out_specs = [row(ATTN_W), row(KV_W), row(KV_W), row(GMLP_W)]
    if emit_kv:
        out_shape += [jax.ShapeDtypeStruct((t, KV_W), F32)] * 2
        out_specs += [row(KV_W), row(KV_W)]
    return pl.pallas_call(
        functools.partial(_inproj_kernel, rope=rope, emit_kv=emit_kv),
        out_shape=out_shape,
        grid=(t // TM,),
        in_specs=in_specs,
        out_specs=out_specs,
        scratch_shapes=[pltpu.VMEM((TM, GMLP_W), F32)],
        compiler_params=pltpu.CompilerParams(dimension_semantics=("arbitrary",),
                                             vmem_limit_bytes=48 * MIB),
        name="inproj_rope" if rope else "inproj",
    )(*args)


def _attn_kernel(q_ref, k_ref, v_ref, gain_ref, o_ref):
    outs = [None] * N_HEADS
    for kvh in range(KV_HEADS):
        ks = slice(kvh * HEAD_DIM, (kvh + 1) * HEAD_DIM)
        kh = k_ref[:, ks]
        vh = v_ref[:, ks]
        q4 = jnp.concatenate(
            [q_ref[:, (kvh * GQA_GROUP + g) * HEAD_DIM:(kvh * GQA_GROUP + g + 1) * HEAD_DIM]
             for g in range(GQA_GROUP)], axis=0)
        s = lax.dot_general(q4, kh, (((1,), (1,)), ((), ())), preferred_element_type=F32)
        m = jnp.max(s, axis=-1, keepdims=True)
        p = jnp.exp(s - m)
        l = jnp.sum(p, axis=-1, keepdims=True)
        o = jnp.dot(p.astype(BF16), vh, preferred_element_type=F32) / l
        for g in range(GQA_GROUP):
            outs[kvh * GQA_GROUP + g] = o[g * TQ:(g + 1) * TQ]
    ss = outs[0] * outs[0]
    for hh in range(1, N_HEADS):
        ss = ss + outs[hh] * outs[hh]
    inv = lax.rsqrt(jnp.sum(ss, axis=-1, keepdims=True) * (1.0 / ATTN_W) + EPS)
    gain = gain_ref[...]
    for hh in range(N_HEADS):
        sl = slice(hh * HEAD_DIM, (hh + 1) * HEAD_DIM)
        o_ref[:, sl] = (outs[hh] * inv * gain[:, sl]).astype(BF16)


def _attention(q2d, k3d, v3d, gain, seq_len):
    t = q2d.shape[0]
    b, l, _ = k3d.shape
    nq = seq_len // TQ
    return pl.pallas_call(
        _attn_kernel,
        out_shape=jax.ShapeDtypeStruct((t, ATTN_W), BF16),
        grid=(b, nq),
        in_specs=[pl.BlockSpec((TQ, ATTN_W), lambda bi, i: (bi * nq + i, 0)),
                  pl.BlockSpec((None, l, KV_W), lambda bi, i: (bi, 0, 0)),
                  pl.BlockSpec((None, l, KV_W), lambda bi, i: (bi, 0, 0)),
                  pl.BlockSpec((1, ATTN_W), lambda bi, i: (0, 0))],
        out_specs=pl.BlockSpec((TQ, ATTN_W), lambda bi, i: (bi * nq + i, 0)),
        compiler_params=pltpu.CompilerParams(dimension_semantics=("arbitrary", "arbitrary"),
                                             vmem_limit_bytes=48 * MIB),
        name="attention",
    )(q2d, k3d, v3d, gain)


def _outproj_kernel(ap_ref, as_ref, gmp_ref, gms_ref, w_ref, xp_ref, xs_ref, g1_ref, n2_ref, sc2_ref,
                    sh2_ref, wr_ref, br_ref, x1_out, h2_out, idx_out, gate_out, rank_out, cnt_out,
                    cnt_scr, *, prompt_tiles):
    i = pl.program_id(0)
    is_p = i < prompt_tiles

    @pl.when(i == 0)
    def _():
        cnt_scr[...] = jnp.zeros_like(cnt_scr)

    a = jnp.where(is_p, ap_ref[...], as_ref[...])
    gm = jnp.where(is_p, gmp_ref[...], gms_ref[...])
    x = jnp.where(is_p, xp_ref[...], xs_ref[...])
    o = (jnp.dot(a, w_ref[0:ATTN_W, :], preferred_element_type=F32)
         + jnp.dot(gm, w_ref[ATTN_W:, :], preferred_element_type=F32))
    x1 = x + g1_ref[...] * o
    x1_out[...] = x1
    h2 = _rms(x1) * (n2_ref[...] * (1.0 + sc2_ref[...])) + sh2_ref[...]
    for s in range(ROW_TILES):
        h2_out[pl.ds(s, TM, stride=ROW_TILES), :] = h2[:, s * LANES:(s + 1) * LANES]

    h_hi = h2.astype(BF16)
    h_lo = (h2 - h_hi.astype(F32)).astype(BF16)
    wr = wr_ref[...]
    w_hi = wr.astype(BF16)
    w_lo = (wr - w_hi.astype(F32)).astype(BF16)
    logits = (jnp.dot(h_hi, w_hi, preferred_element_type=F32)
              + jnp.dot(h_lo, w_hi, preferred_element_type=F32)
              + jnp.dot(h_hi, w_lo, preferred_element_type=F32)) + br_ref[...]

    lane = lax.broadcasted_iota(jnp.int32, (TM, LANES), 1)
    lane_f = lane.astype(F32)
    work = logits
    vals, sels = [], []
    multi = jnp.zeros((TM, LANES), F32)
    idx_acc = jnp.zeros((TM, LANES), F32)
    for k in range(TOP_K):
        m = jnp.max(work, axis=-1, keepdims=True)
        idx = jnp.min(jnp.where(work == m, lane_f, float(LANES)), axis=-1, keepdims=True)
        sel = lane_f == idx
        vals.append(m)
        sels.append(sel)
        work = jnp.where(sel, -jnp.inf, work)
        multi = multi + sel.astype(F32)
        idx_acc = jnp.where(lane == k, idx, idx_acc)
    es = [jnp.exp(v - vals[0]) for v in vals]
    den = es[0] + es[1] + es[2] + es[3]
    gate_acc = jnp.zeros((TM, LANES), F32)
    for k in range(TOP_K):
        gate_acc = jnp.where(lane == k, es[k] / den, gate_acc)

    r_i = lax.broadcasted_iota(jnp.int32, (TM, TM), 0)
    c_i = lax.broadcasted_iota(jnp.int32, (TM, TM), 1)
    lower = (c_i < r_i).astype(BF16)
    prefix = jnp.dot(lower, multi.astype(BF16), preferred_element_type=F32) + cnt_scr[...]
    rank_acc = jnp.zeros((TM, LANES), F32)
    for k in range(TOP_K):
        r = jnp.sum(jnp.where(sels[k], prefix, 0.0), axis=-1, keepdims=True)
        rank_acc = jnp.where(lane == k, r, rank_acc)
    cnt_new = cnt_scr[...] + jnp.sum(multi, axis=0, keepdims=True)
    cnt_scr[...] = cnt_new
    cnt_out[...] = cnt_new
    idx_out[...] = idx_acc.astype(jnp.int32)
    gate_out[...] = gate_acc
    rank_out[...] = rank_acc.astype(jnp.int32)


def _outproj(attn_p, attn_s, gm_p, gm_s, w_out_bf, xp, xs, mod4, row_fn, norm2, wr_pad, br_pad):
    ntp = xp.shape[0] // TM
    t = xp.shape[0] + xs.shape[0]
    const = lambda i: (0, 0)
    row = lambda w: pl.BlockSpec((TM, w), lambda i: (i, 0))
    row_p = lambda w: pl.BlockSpec((TM, w), lambda i: (jnp.minimum(i, ntp - 1), 0))
    row_s = lambda w: pl.BlockSpec((TM, w), lambda i: (jnp.maximum(i - ntp, 0), 0))
    return pl.pallas_call(
        functools.partial(_outproj_kernel, prompt_tiles=ntp),
        out_shape=[jax.ShapeDtypeStruct((t, D_MODEL), F32),
                   jax.ShapeDtypeStruct((t * ROW_TILES, LANES), F32),
                   jax.ShapeDtypeStruct((t, LANES), jnp.int32),
                   jax.ShapeDtypeStruct((t, LANES), F32),
                   jax.ShapeDtypeStruct((t, LANES), jnp.int32),
                   jax.ShapeDtypeStruct((1, LANES), F32)],
        grid=(t // TM,),
        in_specs=[row_p(ATTN_W), row_s(ATTN_W), row_p(GMLP_W), row_s(GMLP_W),
                  pl.BlockSpec((D_MODEL, D_MODEL), const, pipeline_mode=pl.Buffered(1)),
                  row_p(D_MODEL), row_s(D_MODEL),
                  _mod_spec(2, row_fn),
                  pl.BlockSpec((1, D_MODEL), const),
                  _mod_spec(4, row_fn), _mod_spec(3, row_fn),
                  pl.BlockSpec((D_MODEL, LANES), const),
                  pl.BlockSpec((1, LANES), const)],
        out_specs=[row(D_MODEL),
                   pl.BlockSpec((TM * ROW_TILES, LANES), lambda i: (i, 0)),
                   row(LANES), row(LANES), row(LANES),
                   pl.BlockSpec((1, LANES), const)],
        scratch_shapes=[pltpu.VMEM((1, LANES), F32)],
        compiler_params=pltpu.CompilerParams(dimension_semantics=("arbitrary",),
                                             vmem_limit_bytes=48 * MIB),
        name="outproj_router",
    )(attn_p, attn_s, gm_p, gm_s, w_out_bf, xp, xs, mod4, norm2, mod4, mod4, wr_pad, br_pad)


def _moe_kernel(ie_ref, irow_ref, insb_ref, inv_ref, tok_ref, aid_ref,
                h2_hbm, wg_hbm, wu_hbm, wd_hbm, bg_ref, bu_ref, bd_ref,
                y_hbm,
                x2d, act_buf, wdb, wgus, wds, wgub, ystage, gstage, sem_g, sem_w, sem_y):
    g = pl.program_id(0)
    nsb = insb_ref[g]
    e = ie_ref[g]
    row0 = irow_ref[g]
    n_valid = inv_ref[g]
    nsb_next = insb_ref[g + 1]
    row0_next = irow_ref[g + 1]
    e_next = ie_ref[g + 1]
    need = nsb_next * (SB // GCH)
    nsteps = nsb * (NF + NN)
    n_chunks = jnp.maximum(need, nsteps + LAG)
    xs_cur = g % 2
    xs_next = 1 - xs_cur

    def issue_chunk(c):
        slot = c % RING
        valid = c < need
        for j in range(GCH):
            tok = tok_ref[jnp.where(valid, row0_next + c * GCH + j, 0)]
            pltpu.make_async_copy(
                h2_hbm.at[pl.ds(_aligned(tok * ROW_TILES, ROW_TILES), ROW_TILES)],
                gstage.at[pl.ds(_aligned((slot * GCH + j) * ROW_TILES, ROW_TILES), ROW_TILES)],
                sem_g.at[slot]).start()

    def retire_chunk(c):
        slot = c % RING
        sbase = _aligned(slot * (GCH * ROW_TILES), GCH * ROW_TILES)
        pltpu.make_async_copy(h2_hbm.at[pl.ds(0, GCH * ROW_TILES)],
                              gstage.at[pl.ds(sbase, GCH * ROW_TILES)], sem_g.at[slot]).wait()
        dst = _aligned(jnp.where(c < need, c * GCH, RB), GCH)
        for s in range(ROW_TILES):
            x2d[xs_next, pl.ds(dst, GCH), s * LANES:(s + 1) * LANES] = (
                gstage[pl.ds(sbase + s, GCH, stride=ROW_TILES), :].astype(BF16))

    def step(t):
        issue_chunk(t + LAG)
        retire_chunk(t)

    def w_copies(ex, f, slot):
        cols = pl.ds(f * TF, TF)
        return (pltpu.make_async_copy(wg_hbm.at[ex, :, cols], wgus.at[slot, 0], sem_w.at[slot, 0]),
                pltpu.make_async_copy(wu_hbm.at[ex, :, cols], wgus.at[slot, 1], sem_w.at[slot, 1]),
                pltpu.make_async_copy(wd_hbm.at[ex, cols, :], wds.at[slot], sem_w.at[slot, 2]))

    def y_copy(slot, j, aid):
        return pltpu.make_async_copy(
            ystage.at[slot, pl.ds(_aligned(j * ROW_TILES, ROW_TILES), ROW_TILES)],
            y_hbm.at[pl.ds(_aligned(aid * ROW_TILES, ROW_TILES), ROW_TILES)], sem_y.at[slot])

    def y_wait(slot, n_rows=SB):
        n = n_rows * ROW_TILES
        pltpu.make_async_copy(ystage.at[slot, pl.ds(0, n)], y_hbm.at[pl.ds(0, n)], sem_y.at[slot]).wait()

    def compute_item():
        @pl.when(g == 1)
        def _():
            for cp in w_copies(e, 0, 0):
                cp.start()

        for f in range(NF):
            slot = f % 2
            if f + 1 < NF:
                for cp in w_copies(e, f + 1, 1 - slot):
                    cp.start()
            for cp in w_copies(e, f, slot):
                cp.wait()
            wgub[:, 0:TF] = wgus[slot, 0].astype(BF16)
            wgub[:, TF:2 * TF] = wgus[slot, 1].astype(BF16)
            wdb[f * TF:(f + 1) * TF, :] = wds[slot].astype(BF16)
            fs = slice(f * TF, (f + 1) * TF)
            b_gu = jnp.concatenate([bg_ref[pl.ds(e, 1), fs], bu_ref[pl.ds(e, 1), fs]], axis=1)

            def gate_up(sb, carry, f=f, b_gu=b_gu):
                rows = pl.ds(_aligned(sb * SB, SB), SB)
                gu = jnp.dot(x2d[xs_cur, rows, :], wgub[...], preferred_element_type=F32) + b_gu
                gt = jnp.minimum(gu[:, 0:TF], SWIGLU_LIMIT)
                up = jnp.clip(gu[:, TF:2 * TF], -SWIGLU_LIMIT, SWIGLU_LIMIT)
                act = (up + 1.0) * (gt * jax.nn.sigmoid(SWIGLU_ALPHA * gt))
                act_buf[rows, f * TF:(f + 1) * TF] = act.astype(BF16)
                step(f * nsb + sb)
                return carry
            lax.fori_loop(0, nsb, gate_up, 0)

        @pl.when(nsb_next > 0)
        def _():
            for cp in w_copies(e_next, 0, 0):
                cp.start()

        def down(sb, scatter_prev):
            yslot = sb % 2
            if scatter_prev:
                @pl.when(sb >= 2)
                def _():
                    y_wait(yslot)
            a = act_buf[pl.ds(_aligned(sb * SB, SB), SB), :]
            ys = ystage.at[yslot]
            for n in range(NN):
                ns = slice(n * TN, (n + 1) * TN)
                yn = jnp.dot(a, wdb[:, ns], preferred_element_type=F32) + bd_ref[pl.ds(e, 1), ns]
                for s in range(TN // LANES):
                    ys[pl.ds(n * (TN // LANES) + s, SB, stride=ROW_TILES), :] = (
                        yn[:, s * LANES:(s + 1) * LANES])
                step(NF * nsb + sb * NN + n)
                if scatter_prev:
                    for j in range(n * YCH, (n + 1) * YCH):
                        y_copy(1 - yslot, j, aid_ref[row0 + (sb - 1) * SB + j]).start()

        down(0, False)

        def down_loop(sb, carry):
            down(sb, True)
            return carry
        lax.fori_loop(1, nsb, down_loop, 0)

        last = nsb - 1
        n_last = n_valid - last * SB

        def issue_y(j, carry):
            y_copy(last % 2, j, aid_ref[row0 + last * SB + j]).start()
            return carry
        lax.fori_loop(0, n_last, issue_y, 0)
        y_wait(last % 2, n_last)

        @pl.when(nsb >= 2)
        def _():
            y_wait(nsb % 2)

    @pl.when((nsb > 0) | (need > 0))
    def _():
        for c in range(LAG):
            issue_chunk(c)
        pl.when(nsb > 0)(compute_item)

        def catch_up(c, carry):
            retire_chunk(c)

            @pl.when(c + LAG < n_chunks)
            def _():
                issue_chunk(c + LAG)
            return carry
        lax.fori_loop(nsteps, n_chunks, catch_up, 0)


def _moe(item_e, item_row, item_nsb, item_nvalid, tok_sorted, aid_sorted, h2g, w_gate, w_up, w_down,
         b_gate, b_up, b_down, n_items, n_out_rows):
    any_spec = pl.BlockSpec(memory_space=pl.ANY)
    bias_spec = pl.BlockSpec((N_EXPERTS, D_MODEL), lambda i, *_: (0, 0))
    grid_spec = pltpu.PrefetchScalarGridSpec(
        num_scalar_prefetch=6,
        grid=(n_items + 1,),
        in_specs=[any_spec, any_spec, any_spec, any_spec, bias_spec, bias_spec, bias_spec],
        out_specs=any_spec,
        scratch_shapes=[pltpu.VMEM((2, RB + GCH, D_MODEL), BF16),
                        pltpu.VMEM((RB, D_FF), BF16),
                        pltpu.VMEM((D_FF, D_MODEL), BF16),
                        pltpu.VMEM((2, 2, D_MODEL, TF), F32),
                        pltpu.VMEM((2, TF, D_MODEL), F32),
                        pltpu.VMEM((D_MODEL, 2 * TF), BF16),
                        pltpu.VMEM((2, SB * ROW_TILES, LANES), F32),
                        pltpu.VMEM((RING * GCH * ROW_TILES, LANES), F32),
                        pltpu.SemaphoreType.DMA((RING,)),
                        pltpu.SemaphoreType.DMA((2, 3)),
                        pltpu.SemaphoreType.DMA((2,))],
    )
    return pl.pallas_call(
        _moe_kernel,
        out_shape=jax.ShapeDtypeStruct((n_out_rows * ROW_TILES, LANES), F32),
        grid_spec=grid_spec,
        compiler_params=pltpu.CompilerParams(dimension_semantics=("arbitrary",),
                                             vmem_limit_bytes=58 * MIB),
        name="moe_experts",
    )(item_e, item_row, item_nsb, item_nvalid, tok_sorted, aid_sorted, h2g, w_gate, w_up, w_down,
      b_gate, b_up, b_down)


def _combine_kernel(y0, y1, y2, y3, gate_ref, x1_ref, g2_ref, nf_ref, o_ref, f_scr):
    gate = gate_ref[...]
    for k, yk in enumerate((y0, y1, y2, y3)):
        gk = gate[:, k:k + 1]
        for s in range(ROW_TILES):
            sl = slice(s * LANES, (s + 1) * LANES)
            part = gk * yk[pl.ds(s, TC, stride=ROW_TILES), :]
            if k == 0:
                f_scr[:, sl] = part
            else:
                f_scr[:, sl] += part
    x2 = x1_ref[...] + g2_ref[...] * f_scr[...]
    o_ref[...] = _rms(x2) * nf_ref[...]


def _combine(ytok, gates, x1, mod4, row_fn, norm_f, tok_off, t, t_total):
    blocks_total = t_total // TC
    off = tok_off // TC

    def y_spec(k):
        return pl.BlockSpec((TC * ROW_TILES, LANES), lambda i: (k * blocks_total + off + i, 0))
    row_in = lambda w: pl.BlockSpec((TC, w), lambda i: (off + i, 0))
    row = lambda w: pl.BlockSpec((TC, w), lambda i: (i, 0))
    return pl.pallas_call(
        _combine_kernel,
        out_shape=jax.ShapeDtypeStruct((t, D_MODEL), F32),
        grid=(t // TC,),
        in_specs=[y_spec(0), y_spec(1), y_spec(2), y_spec(3), row_in(LANES), row_in(D_MODEL),
                  pl.BlockSpec((None, None, 1, D_MODEL),
                               lambda i: (row_fn(i * TC // TM), 5, 0, 0)),
                  pl.BlockSpec((1, D_MODEL), lambda i: (0, 0))],
        out_specs=row(D_MODEL),
        scratch_shapes=[pltpu.VMEM((TC, D_MODEL), F32)],
        compiler_params=pltpu.CompilerParams(dimension_semantics=("arbitrary",),
                                             vmem_limit_bytes=40 * MIB),
        name="combine_norm",
    )(ytok, ytok, ytok, ytok, gates, x1, mod4, norm_f)


def _rope_tables(n_tokens):
    n_rows = n_tokens // GRID_W
    row = jnp.repeat(jnp.arange(n_rows, dtype=F32), GRID_W)
    col = jnp.tile(jnp.arange(GRID_W, dtype=F32), n_rows)
    half = AXIS_DIM // 2
    inv_freq = ROPE_THETA ** (-jnp.arange(half, dtype=F32) / half)
    ang_r = row[:, None] * inv_freq[None, :]
    ang_c = col[:, None] * inv_freq[None, :]
    cos_t = jnp.concatenate([jnp.cos(ang_r)] * 2 + [jnp.cos(ang_c)] * 2, axis=-1)
    sin_t = jnp.concatenate([-jnp.sin(ang_r), jnp.sin(ang_r), -jnp.sin(ang_c), jnp.sin(ang_c)], axis=-1)
    return cos_t, sin_t


def kernel(x_prompt, x_sample, cache_k, cache_v, c, c_ctx, w_mod, b_mod, norm1, w_in, q_gain, k_gain,
           w_sp, b_sp, attn_out_gain, gmlp_out_gain, w_out, norm2, w_router, b_router,
           w_gate, b_gate, w_up, b_up, w_down, b_down, norm_f):
    bp, sp_len, _ = x_prompt.shape
    bs, ss_len, _ = x_sample.shape
    depth = w_mod.shape[0]
    n_p = bp * sp_len
    n_s = bs * ss_len
    t_total = n_p + n_s
    n_assign = t_total * TOP_K
    n_sub = (n_assign + N_EXPERTS * (SB - 1)) // SB + 1
    n_pad = n_sub * SB
    n_items = N_EXPERTS + -(-n_pad // RB)

    xp = x_prompt.reshape(n_p, D_MODEL)
    xs = x_sample.reshape(n_s, D_MODEL)
    tiles_per_sample = ss_len // TM
    row_p = lambda i: 0
    row_s = lambda i: 1 + i // tiles_per_sample
    row_all = lambda i: jnp.where(i < n_p // TM, 0, 1 + (i - n_p // TM) // tiles_per_sample)
    rope_tabs = _rope_tables(ss_len)
    cvec8 = jnp.zeros((8, D_MODEL), F32).at[0].set(c_ctx).at[1:1 + bs].set(c)
    new_k, new_v = [], []

    for l in range(depth):
        mod4 = _modulation(cvec8, w_mod[l], b_mod[l]).reshape(8, N_MOD, 1, D_MODEL)
        w_in_bf = w_in[l].astype(BF16)
        w_out_bf = w_out[l].astype(BF16)
        wsp_bf = w_sp[l].astype(BF16)
        bsp_b = jnp.broadcast_to(b_sp[l][:, :, None], (G_HEADS, CHUNK, G_DIM))
        n1 = norm1[l].reshape(1, D_MODEL)
        n2 = norm2[l].reshape(1, D_MODEL)
        qg = q_gain[l].reshape(1, HEAD_DIM)
        kg = k_gain[l].reshape(1, HEAD_DIM)
        gog = gmlp_out_gain[l].reshape(1, GMLP_W)
        aog = attn_out_gain[l].reshape(1, ATTN_W)
        wr_pad = jnp.zeros((D_MODEL, LANES), F32).at[:, :N_EXPERTS].set(w_router[l])
        br_pad = jnp.full((1, LANES), NEG, F32).at[0, :N_EXPERTS].set(b_router[l])

        qp, kp, vp, gmp, kpf, vpf = _inproj(xp, mod4, row_p, n1, w_in_bf, qg, kg, None, wsp_bf, bsp_b,
                                            gog, sp_len, True)
        qs, ks, vs, gms = _inproj(xs, mod4, row_s, n1, w_in_bf, qg, kg, rope_tabs, wsp_bf, bsp_b,
                                  gog, ss_len, False)
        new_k.append(kpf.reshape(bp, sp_len, KV_HEADS, HEAD_DIM))
        new_v.append(vpf.reshape(bp, sp_len, KV_HEADS, HEAD_DIM))

        attn_p = _attention(qp, kp.reshape(bp, sp_len, KV_W), vp.reshape(bp, sp_len, KV_W), aog, sp_len)
        k_all = jnp.concatenate([cache_k[:, l].reshape(bs, -1, KV_W).astype(BF16),
                                 ks.reshape(bs, ss_len, KV_W)], axis=1)
        v_all = jnp.concatenate([cache_v[:, l].reshape(bs, -1, KV_W).astype(BF16),
                                 vs.reshape(bs, ss_len, KV_W)], axis=1)
        attn_s = _attention(qs, k_all, v_all, aog, ss_len)

        x1, h2g, idx_l, gates, rank_l, cnt = _outproj(attn_p, attn_s, gmp, gms, w_out_bf, xp, xs, mod4,
                                                      row_all, n2, wr_pad, br_pad)

        idx = idx_l[:, :TOP_K]
        rank = rank_l[:, :TOP_K]
        counts = cnt[0, :N_EXPERTS].astype(jnp.int32)
        padded = ((counts + SB - 1) // SB) * SB
        pad_end = jnp.cumsum(padded)
        pad_start = pad_end - padded
        dest = pad_start[idx] + rank
        slot_id = (jnp.arange(TOP_K, dtype=jnp.int32)[None, :] * t_total
                   + jnp.arange(t_total, dtype=jnp.int32)[:, None])
        aid_sorted = jnp.full((n_pad,), -1, jnp.int32).at[dest.reshape(-1)].set(slot_id.reshape(-1))
        is_pad = aid_sorted < 0
        tok_sorted = jnp.where(is_pad, 0, aid_sorted % t_total)
        aid_sorted = jnp.where(is_pad, 0, aid_sorted)
        items_e = (padded + RB - 1) // RB
        items_end = jnp.cumsum(items_e)
        j = jnp.arange(n_items, dtype=jnp.int32)
        item_e = jnp.minimum(jnp.sum((items_end[None, :] <= j[:, None]).astype(jnp.int32), axis=1),
                             N_EXPERTS - 1)
        within = j - (items_end - items_e)[item_e]
        item_row = pad_start[item_e] + within * RB
        item_rows = jnp.clip(padded[item_e] - within * RB, 0, RB)
        item_nsb = jnp.where(j < items_end[-1], item_rows // SB, 0).astype(jnp.int32)
        item_row = jnp.where(item_nsb > 0, item_row, 0).astype(jnp.int32)
        item_nvalid = jnp.clip(counts[item_e] - within * RB, 0, item_nsb * SB).astype(jnp.int32)

        framed = lambda a: jnp.pad(a.astype(jnp.int32), (1, 1))
        ytok = _moe(framed(item_e), framed(item_row), framed(item_nsb), framed(item_nvalid),
                    tok_sorted, aid_sorted, h2g, w_gate[l], w_up[l], w_down[l],
                    b_gate[l], b_up[l], b_down[l], n_items, n_assign)

        nf = norm_f.reshape(1, D_MODEL)
        is_last = l == depth - 1
        assert is_last, "only DEPTH == 1 is supported"
        y_prompt = _combine(ytok, gates, x1, mod4, row_p, nf, 0, n_p, t_total)
        y_sample = _combine(ytok, gates, x1, mod4, row_s, nf, n_p, n_s, t_total)

    return (y_prompt.reshape(x_prompt.shape), y_sample.reshape(x_sample.shape),
            jnp.stack(new_k, axis=1), jnp.stack(new_v, axis=1))
```

```python
import functools

import jax
import jax.numpy as jnp
from jax import lax
from jax.experimental import pallas as pl
from jax.experimental.pallas import tpu as pltpu

F32 = jnp.float32
BF16 = jnp.bfloat16

D_MODEL = 2048
GRID_W = 64
N_HEADS = 8
KV_HEADS = 2
HEAD_DIM = 128
GQA_GROUP = N_HEADS // KV_HEADS
ATTN_W = N_HEADS * HEAD_DIM
KV_W = KV_HEADS * HEAD_DIM
G_HEADS = 8
G_DIM = (D_MODEL - ATTN_W) // G_HEADS
GMLP_W = G_HEADS * G_DIM
IN_W = ATTN_W + 2 * KV_W + 2 * GMLP_W
CHUNK = 128
ROPE_THETA = 10000.0
AXIS_DIM = HEAD_DIM // 2
N_EXPERTS = 32
TOP_K = 4
D_FF = D_MODEL
SWIGLU_ALPHA = 1.702
SWIGLU_LIMIT = 7.0
N_MOD = 6
EPS = 1e-6

LANES = 128
MIB = 1024 * 1024

TM = 256
TQ = 128
TC = 128
MOD_TN = 1024
SB = 256
RB = 1792
TF = 256
NF = D_FF // TF
GCH = 128
CH = 1
LAG = CH
RING = 2 * CH
BIG_NSB = (RB // SB, RB // SB - 1)
NEG = -1e30


def _rms(x):
    return x * lax.rsqrt(jnp.mean(x * x, axis=-1, keepdims=True) + EPS)


def _aligned(x, m):
    if isinstance(x, int):
        assert x % m == 0
        return x
    return pl.multiple_of(x, m)


def _mod_kernel(c_ref, w_ref, b_ref, o_ref):
    c = c_ref[...]
    a = c * jax.nn.sigmoid(c)
    o_ref[...] = jnp.dot(a, w_ref[...], preferred_element_type=F32,
                         precision=lax.Precision.HIGHEST) + b_ref[...]


def _modulation(cvec8, w_mod, b_mod):
    n = w_mod.shape[1]
    return pl.pallas_call(
        _mod_kernel,
        out_shape=jax.ShapeDtypeStruct((8, n), F32),
        grid=(n // MOD_TN,),
        in_specs=[pl.BlockSpec((8, D_MODEL), lambda j: (0, 0)),
                  pl.BlockSpec((D_MODEL, MOD_TN), lambda j: (0, j)),
                  pl.BlockSpec((1, MOD_TN), lambda j: (0, j))],
        out_specs=pl.BlockSpec((8, MOD_TN), lambda j: (0, j)),
        compiler_params=pltpu.CompilerParams(dimension_semantics=("arbitrary",),
                                             vmem_limit_bytes=40 * MIB),
        name="modulation",
    )(cvec8, w_mod, b_mod.reshape(1, n))


def _mod_spec(part, row_fn):
    return pl.BlockSpec((None, None, 1, D_MODEL), lambda i: (row_fn(i), part, 0, 0))


def _inproj_kernel(*refs, rope, emit_kv):
    it = iter(refs)
    x_ref, shift_ref, scale_ref, norm_ref, w_ref, qg_ref, kg_ref = (next(it) for _ in range(7))
    cos_ref = sin_ref = None
    if rope:
        cos_ref, sin_ref = next(it), next(it)
    wsp_ref, bsp_ref, gog_ref = next(it), next(it), next(it)
    q_out, k_out, v_out, gm_out = (next(it) for _ in range(4))
    kf_out = vf_out = None
    if emit_kv:
        kf_out, vf_out = next(it), next(it)
    gm_scr = next(it)

    x = x_ref[...]
    h = _rms(x) * (norm_ref[...] * (1.0 + scale_ref[...])) + shift_ref[...]
    z = jnp.dot(h.astype(BF16), w_ref[...], preferred_element_type=F32)

    if rope:
        cosv = cos_ref[...]
        sinv = sin_ref[...]
        lane = lax.broadcasted_iota(jnp.int32, (TM, HEAD_DIM), 1)
        first_half = (lane % AXIS_DIM) < (AXIS_DIM // 2)

    def head(zc, gain):
        n = _rms(zc) * gain
        if rope:
            partner = jnp.where(first_half, pltpu.roll(n, HEAD_DIM - AXIS_DIM // 2, 1),
                                pltpu.roll(n, AXIS_DIM // 2, 1))
            n = n * cosv + partner * sinv
        return n

    qg = qg_ref[...]
    kg = kg_ref[...]
    sm_scale = HEAD_DIM ** -0.5
    for hh in range(N_HEADS):
        sl = slice(hh * HEAD_DIM, (hh + 1) * HEAD_DIM)
        q_out[:, sl] = (head(z[:, sl], qg) * sm_scale).astype(BF16)
    for hh in range(KV_HEADS):
        sl = slice(hh * HEAD_DIM, (hh + 1) * HEAD_DIM)
        kh = head(z[:, ATTN_W + hh * HEAD_DIM:ATTN_W + (hh + 1) * HEAD_DIM], kg)
        k_out[:, sl] = kh.astype(BF16)
        if emit_kv:
            kf_out[:, sl] = kh
    v = z[:, ATTN_W + KV_W:ATTN_W + 2 * KV_W]
    v_out[...] = v.astype(BF16)
    if emit_kv:
        vf_out[...] = v

    u0 = ATTN_W + 2 * KV_W
    g0 = u0 + GMLP_W
    for hh in range(G_HEADS):
        sl = slice(hh * G_DIM, (hh + 1) * G_DIM)
        u = jax.nn.gelu(z[:, u0 + hh * G_DIM:u0 + (hh + 1) * G_DIM])
        g = _rms(jax.nn.gelu(z[:, g0 + hh * G_DIM:g0 + (hh + 1) * G_DIM])).astype(BF16)
        w_h = wsp_ref[hh]
        b_h = bsp_ref[hh]
        for c in range(TM // CHUNK):
            rs = slice(c * CHUNK, (c + 1) * CHUNK)
            sp = jnp.dot(w_h, g[rs], preferred_element_type=F32) + b_h
            gm_scr[rs, sl] = u[rs] * sp
    gm_out[...] = (_rms(gm_scr[...]) * gog_ref[...]).astype(BF16)


def _inproj(x2d, mod4, row_fn, norm1, w_in_bf, q_gain, k_gain, rope_tabs, wsp_bf, bsp_b, gog,
            seq_len, emit_kv):
    t = x2d.shape[0]
    rope = rope_tabs is not None
    tiles_per_seq = seq_len // TM
    const = lambda i: (0, 0)
    in_specs = [pl.BlockSpec((TM, D_MODEL), lambda i: (i, 0)),
                _mod_spec(0, row_fn), _mod_spec(1, row_fn),
                pl.BlockSpec((1, D_MODEL), const),
                pl.BlockSpec((D_MODEL, IN_W), const, pipeline_mode=pl.Buffered(1)),
                pl.BlockSpec((1, HEAD_DIM), const),
                pl.BlockSpec((1, HEAD_DIM), const)]
    args = [x2d, mod4, mod4, norm1, w_in_bf, q_gain, k_gain]
    if rope:
        tab_spec = pl.BlockSpec((TM, HEAD_DIM), lambda i: (i % tiles_per_seq, 0))
        in_specs += [tab_spec, tab_spec]
        args += list(rope_tabs)
    in_specs += [pl.BlockSpec((G_HEADS, CHUNK, CHUNK), lambda i: (0, 0, 0)),
                 pl.BlockSpec((G_HEADS, CHUNK, G_DIM), lambda i: (0, 0, 0)),
                 pl.BlockSpec((1, GMLP_W), const)]
    args += [wsp_bf, bsp_b, gog]
    row = lambda w: pl.BlockSpec((TM, w), lambda i: (i, 0))
    out_shape = [jax.ShapeDtypeStruct((t, ATTN_W), BF16), jax.ShapeDtypeStruct((t, KV_W), BF16),
                 jax.ShapeDtypeStruct((t, KV_W), BF16), jax.ShapeDtypeStruct((t, GMLP_W), BF16)]
    out_specs = [row(ATTN_W), row(KV_W), row(KV_W), row(GMLP_W)]
    if emit_kv:
        out_shape += [jax.ShapeDtypeStruct((t, KV_W), F32)] * 2
        out_specs += [row(KV_W), row(KV_W)]
    return pl.pallas_call(
        functools.partial(_inproj_kernel, rope=rope, emit_kv=emit_kv),
        out_shape=out_shape,
        grid=(t // TM,),
        in_specs=in_specs,
        out_specs=out_specs,
        scratch_shapes=[pltpu.VMEM((TM, GMLP_W), F32)],
        compiler_params=pltpu.CompilerParams(dimension_semantics=("arbitrary",),
                                             vmem_limit_bytes=48 * MIB),
        name="inproj_rope" if rope else "inproj",
    )(*args)


def _attn_kernel(q_ref, k_ref, v_ref, gain_ref, o_ref):
    outs = [None] * N_HEADS
    for kvh in range(KV_HEADS):
        ks = slice(kvh * HEAD_DIM, (kvh + 1) * HEAD_DIM)
        kh = k_ref[:, ks]
        vh = v_ref[:, ks]
        q4 = jnp.concatenate(
            [q_ref[:, (kvh * GQA_GROUP + g) * HEAD_DIM:(kvh * GQA_GROUP + g + 1) * HEAD_DIM]
             for g in range(GQA_GROUP)], axis=0)
        s = lax.dot_general(q4, kh, (((1,), (1,)), ((), ())), preferred_element_type=F32)
        m = jnp.max(s, axis=-1, keepdims=True)
        p = jnp.exp(s - m)
        l = jnp.sum(p, axis=-1, keepdims=True)
        o = jnp.dot(p.astype(BF16), vh, preferred_element_type=F32) / l
        for g in range(GQA_GROUP):
            outs[kvh * GQA_GROUP + g] = o[g * TQ:(g + 1) * TQ]
    ss = outs[0] * outs[0]
    for hh in range(1, N_HEADS):
        ss = ss + outs[hh] * outs[hh]
    inv = lax.rsqrt(jnp.sum(ss, axis=-1, keepdims=True) * (1.0 / ATTN_W) + EPS)
    gain = gain_ref[...]
    for hh in range(N_HEADS):
        sl = slice(hh * HEAD_DIM, (hh + 1) * HEAD_DIM)
        o_ref[:, sl] = (outs[hh] * inv * gain[:, sl]).astype(BF16)


def _attention(q2d, k3d, v3d, gain, seq_len):
    t = q2d.shape[0]
    b, l, _ = k3d.shape
    nq = seq_len // TQ
    return pl.pallas_call(
        _attn_kernel,
        out_shape=jax.ShapeDtypeStruct((t, ATTN_W), BF16),
        grid=(b, nq),
        in_specs=[pl.BlockSpec((TQ, ATTN_W), lambda bi, i: (bi * nq + i, 0)),
                  pl.BlockSpec((None, l, KV_W), lambda bi, i: (bi, 0, 0)),
                  pl.BlockSpec((None, l, KV_W), lambda bi, i: (bi, 0, 0)),
                  pl.BlockSpec((1, ATTN_W), lambda bi, i: (0, 0))],
        out_specs=pl.BlockSpec((TQ, ATTN_W), lambda bi, i: (bi * nq + i, 0)),
        compiler_params=pltpu.CompilerParams(dimension_semantics=("arbitrary", "arbitrary"),
                                             vmem_limit_bytes=48 * MIB),
        name="attention",
    )(q2d, k3d, v3d, gain)


def _outproj_kernel(ap_ref, as_ref, gmp_ref, gms_ref, w_ref, xp_ref, xs_ref, g1_ref, n2_ref, sc2_ref,
                    sh2_ref, wr_ref, br_ref, x1_out, h2_out, idx_out, gate_out, rank_out, cnt_out,
                    cnt_scr, *, prompt_tiles):
    i = pl.program_id(0)
    is_p = i < prompt_tiles

    @pl.when(i == 0)
    def _():
        cnt_scr[...] = jnp.zeros_like(cnt_scr)

    a = jnp.where(is_p, ap_ref[...], as_ref[...])
    gm = jnp.where(is_p, gmp_ref[...], gms_ref[...])
    x = jnp.where(is_p, xp_ref[...], xs_ref[...])
    o = (jnp.dot(a, w_ref[0:ATTN_W, :], preferred_element_type=F32)
         + jnp.dot(gm, w_ref[ATTN_W:, :], preferred_element_type=F32))
    x1 = x + g1_ref[...] * o
    x1_out[...] = x1
    h2 = _rms(x1) * (n2_ref[...] * (1.0 + sc2_ref[...])) + sh2_ref[...]
    h2_out[...] = h2

    h_hi = h2.astype(BF16)
    h_lo = (h2 - h_hi.astype(F32)).astype(BF16)
    wr = wr_ref[...]
    w_hi = wr.astype(BF16)
    w_lo = (wr - w_hi.astype(F32)).astype(BF16)
    logits = (jnp.dot(h_hi, w_hi, preferred_element_type=F32)
              + jnp.dot(h_lo, w_hi, preferred_element_type=F32)
              + jnp.dot(h_hi, w_lo, preferred_element_type=F32)) + br_ref[...]

    lane = lax.broadcasted_iota(jnp.int32, (TM, LANES), 1)
    lane_f = lane.astype(F32)
    work = logits
    vals, sels = [], []
    multi = jnp.zeros((TM, LANES), F32)
    idx_acc = jnp.zeros((TM, LANES), F32)
    for k in range(TOP_K):
        m = jnp.max(work, axis=-1, keepdims=True)
        idx = jnp.min(jnp.where(work == m, lane_f, float(LANES)), axis=-1, keepdims=True)
        sel = lane_f == idx
        vals.append(m)
        sels.append(sel)
        work = jnp.where(sel, -jnp.inf, work)
        multi = multi + sel.astype(F32)
        idx_acc = jnp.where(lane == k, idx, idx_acc)
    es = [jnp.exp(v - vals[0]) for v in vals]
    den = es[0] + es[1] + es[2] + es[3]
    gate_acc = jnp.zeros((TM, LANES), F32)
    for k in range(TOP_K):
        gate_acc = jnp.where(lane == k, es[k] / den, gate_acc)

    r_i = lax.broadcasted_iota(jnp.int32, (TM, TM), 0)
    c_i = lax.broadcasted_iota(jnp.int32, (TM, TM), 1)
    lower = (c_i < r_i).astype(BF16)
    prefix = jnp.dot(lower, multi.astype(BF16), preferred_element_type=F32) + cnt_scr[...]
    rank_acc = jnp.zeros((TM, LANES), F32)
    for k in range(TOP_K):
        r = jnp.sum(jnp.where(sels[k], prefix, 0.0), axis=-1, keepdims=True)
        rank_acc = jnp.where(lane == k, r, rank_acc)
    cnt_new = cnt_scr[...] + jnp.sum(multi, axis=0, keepdims=True)
    cnt_scr[...] = cnt_new
    cnt_out[...] = cnt_new
    idx_out[...] = idx_acc.astype(jnp.int32)
    gate_out[...] = gate_acc
    rank_out[...] = rank_acc.astype(jnp.int32)


def _outproj(attn_p, attn_s, gm_p, gm_s, w_out_bf, xp, xs, mod4, row_fn, norm2, wr_pad, br_pad):
    ntp = xp.shape[0] // TM
    t = xp.shape[0] + xs.shape[0]
    const = lambda i: (0, 0)
    row = lambda w: pl.BlockSpec((TM, w), lambda i: (i, 0))
    row_p = lambda w: pl.BlockSpec((TM, w), lambda i: (jnp.minimum(i, ntp - 1), 0))
    row_s = lambda w: pl.BlockSpec((TM, w), lambda i: (jnp.maximum(i - ntp, 0), 0))
    return pl.pallas_call(
        functools.partial(_outproj_kernel, prompt_tiles=ntp),
        out_shape=[jax.ShapeDtypeStruct((t, D_MODEL), F32),
                   jax.ShapeDtypeStruct((t, D_MODEL), F32),
                   jax.ShapeDtypeStruct((t, LANES), jnp.int32),
                   jax.ShapeDtypeStruct((t, LANES), F32),
                   jax.ShapeDtypeStruct((t, LANES), jnp.int32),
                   jax.ShapeDtypeStruct((1, LANES), F32)],
        grid=(t // TM,),
        in_specs=[row_p(ATTN_W), row_s(ATTN_W), row_p(GMLP_W), row_s(GMLP_W),
                  pl.BlockSpec((D_MODEL, D_MODEL), const, pipeline_mode=pl.Buffered(1)),
                  row_p(D_MODEL), row_s(D_MODEL),
                  _mod_spec(2, row_fn),
                  pl.BlockSpec((1, D_MODEL), const),
                  _mod_spec(4, row_fn), _mod_spec(3, row_fn),
                  pl.BlockSpec((D_MODEL, LANES), const),
                  pl.BlockSpec((1, LANES), const)],
        out_specs=[row(D_MODEL),
                   row(D_MODEL),
                   row(LANES), row(LANES), row(LANES),
                   pl.BlockSpec((1, LANES), const)],
        scratch_shapes=[pltpu.VMEM((1, LANES), F32)],
        compiler_params=pltpu.CompilerParams(dimension_semantics=("arbitrary",),
                                             vmem_limit_bytes=48 * MIB),
        name="outproj_router",
    )(attn_p, attn_s, gm_p, gm_s, w_out_bf, xp, xs, mod4, norm2, mod4, mod4, wr_pad, br_pad)


def _moe_kernel(ie_ref, irow_ref, insb_ref, inv_ref, tok_ref, aid_ref,
                h2_hbm, wg_hbm, wu_hbm, wd_hbm, bgu_ref, bd_ref,
                y_hbm,
                x2d, act_buf, wdb, wgus, wds, wgub, ystage, gstage, sem_g, sem_w, sem_y):
    g = pl.program_id(0)
    nsb = insb_ref[g]
    e = ie_ref[g]
    row0 = irow_ref[g]
    n_valid = inv_ref[g]
    nsb_next = insb_ref[g + 1]
    row0_next = irow_ref[g + 1]
    e_next = ie_ref[g + 1]
    need = nsb_next * (SB // GCH)
    nsteps = jnp.where(nsb > 0, (NF + nsb) * CH, 0)
    n_chunks = jnp.maximum(need, nsteps + LAG)
    xs_cur = g % 2
    xs_next = 1 - xs_cur

    def issue_chunk(c):
        slot = c % RING
        valid = c < need
        for j in range(GCH):
            tok = tok_ref[jnp.where(valid, row0_next + c * GCH + j, 0)]
            pltpu.make_async_copy(h2_hbm.at[pl.ds(tok, 1), :],
                                  gstage.at[pl.ds(slot * GCH + j, 1), :], sem_g.at[slot]).start()

    def retire_chunk(c):
        slot = c % RING
        srows = pl.ds(_aligned(slot * GCH, GCH), GCH)
        pltpu.make_async_copy(h2_hbm.at[pl.ds(0, GCH), :], gstage.at[srows, :], sem_g.at[slot]).wait()
        dst = _aligned(jnp.where(c < need, c * GCH, RB), GCH)
        x2d[xs_next, pl.ds(dst, GCH), :] = gstage[srows, :].astype(BF16)

    def steps(t0):
        for i in range(CH):
            retire_chunk(t0 + i)
            issue_chunk(t0 + i + LAG)

    def w_copies(ex, f, slot):
        cols = pl.ds(_aligned(f * TF, TF), TF)
        return (pltpu.make_async_copy(wg_hbm.at[ex, :, cols], wgus.at[slot, 0], sem_w.at[slot, 0]),
                pltpu.make_async_copy(wu_hbm.at[ex, :, cols], wgus.at[slot, 1], sem_w.at[slot, 1]),
                pltpu.make_async_copy(wd_hbm.at[ex, cols, :], wds.at[slot], sem_w.at[slot, 2]))

    def y_copy(slot, j, aid):
        return pltpu.make_async_copy(ystage.at[slot, pl.ds(j, 1), :], y_hbm.at[pl.ds(aid, 1), :],
                                     sem_y.at[slot])

    def y_wait(slot, n_rows=SB):
        pltpu.make_async_copy(ystage.at[slot, pl.ds(0, n_rows), :], y_hbm.at[pl.ds(0, n_rows), :],
                              sem_y.at[slot]).wait()

    def compute_item():
        @pl.when(g == 1)
        def _():
            for cp in w_copies(e, 0, 0):
                cp.start()

        def gate_up_tile(f, carry):
            slot = f % 2

            @pl.when(f + 1 < NF)
            def _():
                for cp in w_copies(e, f + 1, 1 - slot):
                    cp.start()
            for cp in w_copies(e, f, slot):
                cp.wait()
            wgub[:, 0:TF] = wgus[slot, 0].astype(BF16)
            wgub[:, TF:2 * TF] = wgus[slot, 1].astype(BF16)
            wdb[pl.ds(_aligned(f * TF, TF), TF), :] = wds[slot].astype(BF16)
            b_gu = bgu_ref[pl.ds(e * NF + f, 1), :]

            def gate_up(row_start, m):
                rows = pl.ds(row_start, m)
                gu = jnp.dot(x2d[xs_cur, rows, :], wgub[...], preferred_element_type=F32) + b_gu
                gt = jnp.minimum(gu[:, 0:TF], SWIGLU_LIMIT)
                up = jnp.clip(gu[:, TF:2 * TF], -SWIGLU_LIMIT, SWIGLU_LIMIT)
                act = (up + 1.0) * (gt * jax.nn.sigmoid(SWIGLU_ALPHA * gt))
                act_buf[f, rows, :] = act.astype(BF16)

            for big in BIG_NSB:
                @pl.when(nsb == big)
                def _(big=big):
                    steps(f * CH)
                    gate_up(0, big * SB)

            @pl.when(functools.reduce(lambda a, b: a & b, [nsb != big for big in BIG_NSB]))
            def _():
                def one(sb, c2):
                    gate_up(_aligned(sb * SB, SB), SB)
                    return c2
                steps(f * CH)
                lax.fori_loop(0, nsb, one, 0)
            return carry
        lax.fori_loop(0, NF, gate_up_tile, 0)

        @pl.when(nsb_next > 0)
        def _():
            for cp in w_copies(e_next, 0, 0):
                cp.start()

        def down(sb, scatter_prev):
            yslot = sb % 2
            if scatter_prev:
                @pl.when(sb >= 2)
                def _():
                    y_wait(yslot)
            steps(NF * CH + sb * CH)
            if scatter_prev:
                for j in range(SB):
                    y_copy(1 - yslot, j, aid_ref[row0 + (sb - 1) * SB + j]).start()
            rows = pl.ds(_aligned(sb * SB, SB), SB)
            a = jnp.concatenate([act_buf[f, rows, :] for f in range(NF)], axis=1)
            y = jnp.dot(a, wdb[...], preferred_element_type=F32) + bd_ref[pl.ds(e, 1), :]
            ystage[yslot] = y

        down(0, False)

        def down_loop(sb, carry):
            down(sb, True)
            return carry
        lax.fori_loop(1, nsb, down_loop, 0)

        last = nsb - 1
        n_last = n_valid - last * SB

        def issue_y(j, carry):
            y_copy(last % 2, j, aid_ref[row0 + last * SB + j]).start()
            return carry
        lax.fori_loop(0, n_last, issue_y, 0)
        n_bulk = (n_last // 8) * 8

        @pl.when(n_bulk > 0)
        def _():
            y_wait(last % 2, _aligned(n_bulk, 8))

        def wait_row(j, carry):
            y_wait(last % 2, 1)
            return carry
        lax.fori_loop(n_bulk, n_last, wait_row, 0)

        @pl.when(nsb >= 2)
        def _():
            y_wait(nsb % 2)

    @pl.when((nsb > 0) | (need > 0))
    def _():
        for c in range(LAG):
            issue_chunk(c)
        pl.when(nsb > 0)(compute_item)

        def catch_up(c, carry):
            retire_chunk(c)

            @pl.when(c + LAG < n_chunks)
            def _():
                issue_chunk(c + LAG)
            return carry
        lax.fori_loop(nsteps, n_chunks, catch_up, 0)


def _moe(item_e, item_row, item_nsb, item_nvalid, tok_sorted, aid_sorted, h2g, w_gate, w_up, w_down,
         b_gate, b_up, b_down, n_items, n_out_rows):
    any_spec = pl.BlockSpec(memory_space=pl.ANY)
    b_gu = jnp.concatenate([b_gate.reshape(N_EXPERTS * NF, TF), b_up.reshape(N_EXPERTS * NF, TF)], axis=1)
    grid_spec = pltpu.PrefetchScalarGridSpec(
        num_scalar_prefetch=6,
        grid=(n_items + 1,),
        in_specs=[any_spec, any_spec, any_spec, any_spec,
                  pl.BlockSpec((N_EXPERTS * NF, 2 * TF), lambda i, *_: (0, 0)),
                  pl.BlockSpec((N_EXPERTS, D_MODEL), lambda i, *_: (0, 0))],
        out_specs=any_spec,
        scratch_shapes=[pltpu.VMEM((2, RB + GCH, D_MODEL), BF16),
                        pltpu.VMEM((NF, RB, TF), BF16),
                        pltpu.VMEM((D_FF, D_MODEL), BF16),
                        pltpu.VMEM((2, 2, D_MODEL, TF), F32),
                        pltpu.VMEM((2, TF, D_MODEL), F32),
                        pltpu.VMEM((D_MODEL, 2 * TF), BF16),
                        pltpu.VMEM((2, SB, D_MODEL), F32),
                        pltpu.VMEM((RING * GCH, D_MODEL), F32),
                        pltpu.SemaphoreType.DMA((RING,)),
                        pltpu.SemaphoreType.DMA((2, 3)),
                        pltpu.SemaphoreType.DMA((2,))],
    )
    return pl.pallas_call(
        _moe_kernel,
        out_shape=jax.ShapeDtypeStruct((n_out_rows, D_MODEL), F32),
        grid_spec=grid_spec,
        compiler_params=pltpu.CompilerParams(dimension_semantics=("arbitrary",),
                                             vmem_limit_bytes=58 * MIB),
        name="moe_experts",
    )(item_e, item_row, item_nsb, item_nvalid, tok_sorted, aid_sorted, h2g, w_gate, w_up, w_down,
      b_gu, b_down)


def _combine_kernel(y0, y1, y2, y3, gate_ref, x1_ref, g2_ref, nf_ref, o_ref):
    gate = gate_ref[...]
    f = gate[:, 0:1] * y0[...]
    for k, yk in enumerate((y1, y2, y3), start=1):
        f = f + gate[:, k:k + 1] * yk[...]
    x2 = x1_ref[...] + g2_ref[...] * f
    o_ref[...] = _rms(x2) * nf_ref[...]


def _combine(ytok, gates, x1, mod4, row_fn, norm_f, tok_off, t, t_total):
    blocks_total = t_total // TC
    off = tok_off // TC

    def y_spec(k):
        return pl.BlockSpec((TC, D_MODEL), lambda i: (k * blocks_total + off + i, 0))
    row_in = lambda w: pl.BlockSpec((TC, w), lambda i: (off + i, 0))
    row = lambda w: pl.BlockSpec((TC, w), lambda i: (i, 0))
    return pl.pallas_call(
        _combine_kernel,
        out_shape=jax.ShapeDtypeStruct((t, D_MODEL), F32),
        grid=(t // TC,),
        in_specs=[y_spec(0), y_spec(1), y_spec(2), y_spec(3), row_in(LANES), row_in(D_MODEL),
                  pl.BlockSpec((None, None, 1, D_MODEL),
                               lambda i: (row_fn(i * TC // TM), 5, 0, 0)),
                  pl.BlockSpec((1, D_MODEL), lambda i: (0, 0))],
        out_specs=row(D_MODEL),
        compiler_params=pltpu.CompilerParams(dimension_semantics=("arbitrary",),
                                             vmem_limit_bytes=40 * MIB),
        name="combine_norm",
    )(ytok, ytok, ytok, ytok, gates, x1, mod4, norm_f)


def _rope_tables(n_tokens):
    n_rows = n_tokens // GRID_W
    row = jnp.repeat(jnp.arange(n_rows, dtype=F32), GRID_W)
    col = jnp.tile(jnp.arange(GRID_W, dtype=F32), n_rows)
    half = AXIS_DIM // 2
    inv_freq = ROPE_THETA ** (-jnp.arange(half, dtype=F32) / half)
    ang_r = row[:, None] * inv_freq[None, :]
    ang_c = col[:, None] * inv_freq[None, :]
    cos_t = jnp.concatenate([jnp.cos(ang_r)] * 2 + [jnp.cos(ang_c)] * 2, axis=-1)
    sin_t = jnp.concatenate([-jnp.sin(ang_r), jnp.sin(ang_r), -jnp.sin(ang_c), jnp.sin(ang_c)], axis=-1)
    return cos_t, sin_t


def kernel(x_prompt, x_sample, cache_k, cache_v, c, c_ctx, w_mod, b_mod, norm1, w_in, q_gain, k_gain,
           w_sp, b_sp, attn_out_gain, gmlp_out_gain, w_out, norm2, w_router, b_router,
           w_gate, b_gate, w_up, b_up, w_down, b_down, norm_f):
    bp, sp_len, _ = x_prompt.shape
    bs, ss_len, _ = x_sample.shape
    depth = w_mod.shape[0]
    n_p = bp * sp_len
    n_s = bs * ss_len
    t_total = n_p + n_s
    n_assign = t_total * TOP_K
    n_sub = (n_assign + N_EXPERTS * (SB - 1)) // SB + 1
    n_pad = n_sub * SB
    n_items = N_EXPERTS + -(-n_pad // RB)

    xp = x_prompt.reshape(n_p, D_MODEL)
    xs = x_sample.reshape(n_s, D_MODEL)
    tiles_per_sample = ss_len // TM
    row_p = lambda i: 0
    row_s = lambda i: 1 + i // tiles_per_sample
    row_all = lambda i: jnp.where(i < n_p // TM, 0, 1 + (i - n_p // TM) // tiles_per_sample)
    rope_tabs = _rope_tables(ss_len)
    cvec8 = jnp.zeros((8, D_MODEL), F32).at[0].set(c_ctx).at[1:1 + bs].set(c)
    new_k, new_v = [], []

    for l in range(depth):
        mod4 = _modulation(cvec8, w_mod[l], b_mod[l]).reshape(8, N_MOD, 1, D_MODEL)
        w_in_bf = w_in[l].astype(BF16)
        w_out_bf = w_out[l].astype(BF16)
        wsp_bf = w_sp[l].astype(BF16)
        bsp_b = jnp.broadcast_to(b_sp[l][:, :, None], (G_HEADS, CHUNK, G_DIM))
        n1 = norm1[l].reshape(1, D_MODEL)
        n2 = norm2[l].reshape(1, D_MODEL)
        qg = q_gain[l].reshape(1, HEAD_DIM)
        kg = k_gain[l].reshape(1, HEAD_DIM)
        gog = gmlp_out_gain[l].reshape(1, GMLP_W)
        aog = attn_out_gain[l].reshape(1, ATTN_W)
        wr_pad = jnp.zeros((D_MODEL, LANES), F32).at[:, :N_EXPERTS].set(w_router[l])
        br_pad = jnp.full((1, LANES), NEG, F32).at[0, :N_EXPERTS].set(b_router[l])

        qp, kp, vp, gmp, kpf, vpf = _inproj(xp, mod4, row_p, n1, w_in_bf, qg, kg, None, wsp_bf, bsp_b,
                                            gog, sp_len, True)
        qs, ks, vs, gms = _inproj(xs, mod4, row_s, n1, w_in_bf, qg, kg, rope_tabs, wsp_bf, bsp_b,
                                  gog, ss_len, False)
        new_k.append(kpf.reshape(bp, sp_len, KV_HEADS, HEAD_DIM))
        new_v.append(vpf.reshape(bp, sp_len, KV_HEADS, HEAD_DIM))

        attn_p = _attention(qp, kp.reshape(bp, sp_len, KV_W), vp.reshape(bp, sp_len, KV_W), aog, sp_len)
        k_all = jnp.concatenate([cache_k[:, l].reshape(bs, -1, KV_W).astype(BF16),
                                 ks.reshape(bs, ss_len, KV_W)], axis=1)
        v_all = jnp.concatenate([cache_v[:, l].reshape(bs, -1, KV_W).astype(BF16),
                                 vs.reshape(bs, ss_len, KV_W)], axis=1)
        attn_s = _attention(qs, k_all, v_all, aog, ss_len)

        x1, h2g, idx_l, gates, rank_l, cnt = _outproj(attn_p, attn_s, gmp, gms, w_out_bf, xp, xs, mod4,
                                                      row_all, n2, wr_pad, br_pad)

        idx = idx_l[:, :TOP_K]
        rank = rank_l[:, :TOP_K]
        counts = cnt[0, :N_EXPERTS].astype(jnp.int32)
        padded = ((counts + SB - 1) // SB) * SB
        pad_end = jnp.cumsum(padded)
        pad_start = pad_end - padded
        dest = pad_start[idx] + rank
        slot_id = (jnp.arange(TOP_K, dtype=jnp.int32)[None, :] * t_total
                   + jnp.arange(t_total, dtype=jnp.int32)[:, None])
        aid_sorted = jnp.full((n_pad,), -1, jnp.int32).at[dest.reshape(-1)].set(slot_id.reshape(-1))
        is_pad = aid_sorted < 0
        tok_sorted = jnp.where(is_pad, 0, aid_sorted % t_total)
        aid_sorted = jnp.where(is_pad, 0, aid_sorted)
        items_e = (padded + RB - 1) // RB
        items_end = jnp.cumsum(items_e)
        j = jnp.arange(n_items, dtype=jnp.int32)
        item_e = jnp.minimum(jnp.sum((items_end[None, :] <= j[:, None]).astype(jnp.int32), axis=1),
                             N_EXPERTS - 1)
        within = j - (items_end - items_e)[item_e]
        item_row = pad_start[item_e] + within * RB
        item_rows = jnp.clip(padded[item_e] - within * RB, 0, RB)
        item_nsb = jnp.where(j < items_end[-1], item_rows // SB, 0).astype(jnp.int32)
        item_row = jnp.where(item_nsb > 0, item_row, 0).astype(jnp.int32)
        item_nvalid = jnp.clip(counts[item_e] - within * RB, 0, item_nsb * SB).astype(jnp.int32)

        framed = lambda a: jnp.pad(a.astype(jnp.int32), (1, 1))
        ytok = _moe(framed(item_e), framed(item_row), framed(item_nsb), framed(item_nvalid),
                    tok_sorted, aid_sorted, h2g, w_gate[l], w_up[l], w_down[l],
                    b_gate[l], b_up[l], b_down[l], n_items, n_assign)

        nf = norm_f.reshape(1, D_MODEL)
        is_last = l == depth - 1
        assert is_last, "only DEPTH == 1 is supported"
        y_prompt = _combine(ytok, gates, x1, mod4, row_p, nf, 0, n_p, t_total)
        y_sample = _combine(ytok, gates, x1, mod4, row_s, nf, n_p, n_s, t_total)

    return (y_prompt.reshape(x_prompt.shape), y_sample.reshape(x_sample.shape),
            jnp.stack(new_k, axis=1), jnp.stack(new_v, axis=1))
```

```python
import functools

import jax
import jax.numpy as jnp
from jax import lax
from jax.experimental import pallas as pl
from jax.experimental.pallas import tpu as pltpu

F32 = jnp.float32
BF16 = jnp.bfloat16

D_MODEL = 2048
GRID_W = 64
N_HEADS = 8
KV_HEADS = 2
HEAD_DIM = 128
GQA_GROUP = N_HEADS // KV_HEADS
ATTN_W = N_HEADS * HEAD_DIM
KV_W = KV_HEADS * HEAD_DIM
G_HEADS = 8
G_DIM = (D_MODEL - ATTN_W) // G_HEADS
GMLP_W = G_HEADS * G_DIM
IN_W = ATTN_W + 2 * KV_W + 2 * GMLP_W
CHUNK = 128
ROPE_THETA = 10000.0
AXIS_DIM = HEAD_DIM // 2
N_EXPERTS = 32
TOP_K = 4
D_FF = D_MODEL
SWIGLU_ALPHA = 1.702
SWIGLU_LIMIT = 7.0
N_MOD = 6
EPS = 1e-6

LANES = 128
MIB = 1024 * 1024

TM = 256
TQ = 128
TC = 128
MOD_TN = 1024
SB = 256
RB = 1792
TF = 256
NF = D_FF // TF
BIG_NSB = (RB // SB, RB // SB - 1)
GLAG = 2
GSLOTS = GLAG + 1
NEG = -1e30


def _rms(x):
    return x * lax.rsqrt(jnp.mean(x * x, axis=-1, keepdims=True) + EPS)


def _aligned(x, m):
    if isinstance(x, int):
        assert x % m == 0
        return x
    return pl.multiple_of(x, m)


def _mod_kernel(c_ref, w_ref, b_ref, o_ref):
    c = c_ref[...]
    a = c * jax.nn.sigmoid(c)
    o_ref[...] = jnp.dot(a, w_ref[...], preferred_element_type=F32,
                         precision=lax.Precision.HIGHEST) + b_ref[...]


def _modulation(cvec8, w_mod, b_mod):
    n = w_mod.shape[1]
    return pl.pallas_call(
        _mod_kernel,
        out_shape=jax.ShapeDtypeStruct((8, n), F32),
        grid=(n // MOD_TN,),
        in_specs=[pl.BlockSpec((8, D_MODEL), lambda j: (0, 0)),
                  pl.BlockSpec((D_MODEL, MOD_TN), lambda j: (0, j)),
                  pl.BlockSpec((1, MOD_TN), lambda j: (0, j))],
        out_specs=pl.BlockSpec((8, MOD_TN), lambda j: (0, j)),
        compiler_params=pltpu.CompilerParams(dimension_semantics=("arbitrary",),
                                             vmem_limit_bytes=40 * MIB),
        name="modulation",
    )(cvec8, w_mod, b_mod.reshape(1, n))


def _mod_spec(part, row_fn):
    return pl.BlockSpec((None, None, 1, D_MODEL), lambda i, *_: (row_fn(i), part, 0, 0))


def _inproj_kernel(*refs, rope, emit_kv):
    it = iter(refs)
    x_ref, shift_ref, scale_ref, norm_ref, w_ref, qg_ref, kg_ref = (next(it) for _ in range(7))
    cos_ref = sin_ref = None
    if rope:
        cos_ref, sin_ref = next(it), next(it)
    wsp_ref, bsp_ref, gog_ref = next(it), next(it), next(it)
    q_out, k_out, v_out, gm_out = (next(it) for _ in range(4))
    kf_out = vf_out = None
    if emit_kv:
        kf_out, vf_out = next(it), next(it)
    gm_scr = next(it)

    x = x_ref[...]
    h = _rms(x) * (norm_ref[...] * (1.0 + scale_ref[...])) + shift_ref[...]
    z = jnp.dot(h.astype(BF16), w_ref[...], preferred_element_type=F32)

    if rope:
        cosv = cos_ref[...]
        sinv = sin_ref[...]
        lane = lax.broadcasted_iota(jnp.int32, (TM, HEAD_DIM), 1)
        first_half = (lane % AXIS_DIM) < (AXIS_DIM // 2)

    def head(zc, gain):
        n = _rms(zc) * gain
        if rope:
            partner = jnp.where(first_half, pltpu.roll(n, HEAD_DIM - AXIS_DIM // 2, 1),
                                pltpu.roll(n, AXIS_DIM // 2, 1))
            n = n * cosv + partner * sinv
        return n

    qg = qg_ref[...]
    kg = kg_ref[...]
    sm_scale = HEAD_DIM ** -0.5
    for hh in range(N_HEADS):
        sl = slice(hh * HEAD_DIM, (hh + 1) * HEAD_DIM)
        q_out[:, sl] = (head(z[:, sl], qg) * sm_scale).astype(BF16)
    for hh in range(KV_HEADS):
        sl = slice(hh * HEAD_DIM, (hh + 1) * HEAD_DIM)
        kh = head(z[:, ATTN_W + hh * HEAD_DIM:ATTN_W + (hh + 1) * HEAD_DIM], kg)
        k_out[:, sl] = kh.astype(BF16)
        if emit_kv:
            kf_out[:, sl] = kh
    v = z[:, ATTN_W + KV_W:ATTN_W + 2 * KV_W]
    v_out[...] = v.astype(BF16)
    if emit_kv:
        vf_out[...] = v

    u0 = ATTN_W + 2 * KV_W
    g0 = u0 + GMLP_W
    for hh in range(G_HEADS):
        sl = slice(hh * G_DIM, (hh + 1) * G_DIM)
        u = jax.nn.gelu(z[:, u0 + hh * G_DIM:u0 + (hh + 1) * G_DIM])
        g = _rms(jax.nn.gelu(z[:, g0 + hh * G_DIM:g0 + (hh + 1) * G_DIM])).astype(BF16)
        w_h = wsp_ref[hh]
        b_h = bsp_ref[hh]
        for c in range(TM // CHUNK):
            rs = slice(c * CHUNK, (c + 1) * CHUNK)
            sp = jnp.dot(w_h, g[rs], preferred_element_type=F32) + b_h
            gm_scr[rs, sl] = u[rs] * sp
    gm_out[...] = (_rms(gm_scr[...]) * gog_ref[...]).astype(BF16)


def _inproj(x2d, mod4, row_fn, norm1, w_in_bf, q_gain, k_gain, rope_tabs, wsp_bf, bsp_b, gog,
            seq_len, emit_kv):
    t = x2d.shape[0]
    rope = rope_tabs is not None
    tiles_per_seq = seq_len // TM
    const = lambda i: (0, 0)
    in_specs = [pl.BlockSpec((TM, D_MODEL), lambda i: (i, 0)),
                _mod_spec(0, row_fn), _mod_spec(1, row_fn),
                pl.BlockSpec((1, D_MODEL), const),
                pl.BlockSpec((D_MODEL, IN_W), const, pipeline_mode=pl.Buffered(1)),
                pl.BlockSpec((1, HEAD_DIM), const),
                pl.BlockSpec((1, HEAD_DIM), const)]
    args = [x2d, mod4, mod4, norm1, w_in_bf, q_gain, k_gain]
    if rope:
        tab_spec = pl.BlockSpec((TM, HEAD_DIM), lambda i: (i % tiles_per_seq, 0))
        in_specs += [tab_spec, tab_spec]
        args += list(rope_tabs)
    in_specs += [pl.BlockSpec((G_HEADS, CHUNK, CHUNK), lambda i: (0, 0, 0)),
                 pl.BlockSpec((G_HEADS, CHUNK, G_DIM), lambda i: (0, 0, 0)),
                 pl.BlockSpec((1, GMLP_W), const)]
    args += [wsp_bf, bsp_b, gog]
    row = lambda w: pl.BlockSpec((TM, w), lambda i: (i, 0))
    out_shape = [jax.ShapeDtypeStruct((t, ATTN_W), BF16), jax.ShapeDtypeStruct((t, KV_W), BF16),
                 jax.ShapeDtypeStruct((t, KV_W), BF16), jax.ShapeDtypeStruct((t, GMLP_W), BF16)]
    out_specs = [row(ATTN_W), row(KV_W), row(KV_W), row(GMLP_W)]
    if emit_kv:
        out_shape += [jax.ShapeDtypeStruct((t, KV_W), F32)] * 2
        out_specs += [row(KV_W), row(KV_W)]
    return pl.pallas_call(
        functools.partial(_inproj_kernel, rope=rope, emit_kv=emit_kv),
        out_shape=out_shape,
        grid=(t // TM,),
        in_specs=in_specs,
        out_specs=out_specs,
        scratch_shapes=[pltpu.VMEM((TM, GMLP_W), F32)],
        compiler_params=pltpu.CompilerParams(dimension_semantics=("arbitrary",),
                                             vmem_limit_bytes=48 * MIB),
        name="inproj_rope" if rope else "inproj",
    )(*args)


def _attn_kernel(q_ref, k_ref, v_ref, gain_ref, o_ref):
    outs = [None] * N_HEADS
    for kvh in range(KV_HEADS):
        ks = slice(kvh * HEAD_DIM, (kvh + 1) * HEAD_DIM)
        kh = k_ref[:, ks]
        vh = v_ref[:, ks]
        q4 = jnp.concatenate(
            [q_ref[:, (kvh * GQA_GROUP + g) * HEAD_DIM:(kvh * GQA_GROUP + g + 1) * HEAD_DIM]
             for g in range(GQA_GROUP)], axis=0)
        s = lax.dot_general(q4, kh, (((1,), (1,)), ((), ())), preferred_element_type=F32)
        m = jnp.max(s, axis=-1, keepdims=True)
        p = jnp.exp(s - m)
        l = jnp.sum(p, axis=-1, keepdims=True)
        o = jnp.dot(p.astype(BF16), vh, preferred_element_type=F32) / l
        for g in range(GQA_GROUP):
            outs[kvh * GQA_GROUP + g] = o[g * TQ:(g + 1) * TQ]
    ss = outs[0] * outs[0]
    for hh in range(1, N_HEADS):
        ss = ss + outs[hh] * outs[hh]
    inv = lax.rsqrt(jnp.sum(ss, axis=-1, keepdims=True) * (1.0 / ATTN_W) + EPS)
    gain = gain_ref[...]
    for hh in range(N_HEADS):
        sl = slice(hh * HEAD_DIM, (hh + 1) * HEAD_DIM)
        o_ref[:, sl] = (outs[hh] * inv * gain[:, sl]).astype(BF16)


def _attention(q2d, k3d, v3d, gain, seq_len):
    t = q2d.shape[0]
    b, l, _ = k3d.shape
    nq = seq_len // TQ
    return pl.pallas_call(
        _attn_kernel,
        out_shape=jax.ShapeDtypeStruct((t, ATTN_W), BF16),
        grid=(b, nq),
        in_specs=[pl.BlockSpec((TQ, ATTN_W), lambda bi, i: (bi * nq + i, 0)),
                  pl.BlockSpec((None, l, KV_W), lambda bi, i: (bi, 0, 0)),
                  pl.BlockSpec((None, l, KV_W), lambda bi, i: (bi, 0, 0)),
                  pl.BlockSpec((1, ATTN_W), lambda bi, i: (0, 0))],
        out_specs=pl.BlockSpec((TQ, ATTN_W), lambda bi, i: (bi * nq + i, 0)),
        compiler_params=pltpu.CompilerParams(dimension_semantics=("arbitrary", "arbitrary"),
                                             vmem_limit_bytes=48 * MIB),
        name="attention",
    )(q2d, k3d, v3d, gain)


def _outproj_kernel(ap_ref, as_ref, gmp_ref, gms_ref, w_ref, xp_ref, xs_ref, g1_ref, n2_ref, sc2_ref,
                    sh2_ref, wr_ref, br_ref, x1_out, h2_out, idx_out, gate_out, rank_out, cnt_out,
                    cnt_scr, *, prompt_tiles):
    i = pl.program_id(0)
    is_p = i < prompt_tiles

    @pl.when(i == 0)
    def _():
        cnt_scr[...] = jnp.zeros_like(cnt_scr)

    a = jnp.where(is_p, ap_ref[...], as_ref[...])
    gm = jnp.where(is_p, gmp_ref[...], gms_ref[...])
    x = jnp.where(is_p, xp_ref[...], xs_ref[...])
    o = (jnp.dot(a, w_ref[0:ATTN_W, :], preferred_element_type=F32)
         + jnp.dot(gm, w_ref[ATTN_W:, :], preferred_element_type=F32))
    x1 = x + g1_ref[...] * o
    x1_out[...] = x1
    h2 = _rms(x1) * (n2_ref[...] * (1.0 + sc2_ref[...])) + sh2_ref[...]
    h2_out[...] = h2

    h_hi = h2.astype(BF16)
    h_lo = (h2 - h_hi.astype(F32)).astype(BF16)
    wr = wr_ref[...]
    w_hi = wr.astype(BF16)
    w_lo = (wr - w_hi.astype(F32)).astype(BF16)
    logits = (jnp.dot(h_hi, w_hi, preferred_element_type=F32)
              + jnp.dot(h_lo, w_hi, preferred_element_type=F32)
              + jnp.dot(h_hi, w_lo, preferred_element_type=F32)) + br_ref[...]

    lane = lax.broadcasted_iota(jnp.int32, (TM, LANES), 1)
    lane_f = lane.astype(F32)
    work = logits
    vals, sels = [], []
    multi = jnp.zeros((TM, LANES), F32)
    idx_acc = jnp.zeros((TM, LANES), F32)
    for k in range(TOP_K):
        m = jnp.max(work, axis=-1, keepdims=True)
        idx = jnp.min(jnp.where(work == m, lane_f, float(LANES)), axis=-1, keepdims=True)
        sel = lane_f == idx
        vals.append(m)
        sels.append(sel)
        work = jnp.where(sel, -jnp.inf, work)
        multi = multi + sel.astype(F32)
        idx_acc = jnp.where(lane == k, idx, idx_acc)
    es = [jnp.exp(v - vals[0]) for v in vals]
    den = es[0] + es[1] + es[2] + es[3]
    gate_acc = jnp.zeros((TM, LANES), F32)
    for k in range(TOP_K):
        gate_acc = jnp.where(lane == k, es[k] / den, gate_acc)

    r_i = lax.broadcasted_iota(jnp.int32, (TM, TM), 0)
    c_i = lax.broadcasted_iota(jnp.int32, (TM, TM), 1)
    lower = (c_i < r_i).astype(BF16)
    prefix = jnp.dot(lower, multi.astype(BF16), preferred_element_type=F32) + cnt_scr[...]
    rank_acc = jnp.zeros((TM, LANES), F32)
    for k in range(TOP_K):
        r = jnp.sum(jnp.where(sels[k], prefix, 0.0), axis=-1, keepdims=True)
        rank_acc = jnp.where(lane == k, r, rank_acc)
    cnt_new = cnt_scr[...] + jnp.sum(multi, axis=0, keepdims=True)
    cnt_scr[...] = cnt_new
    cnt_out[...] = cnt_new
    idx_out[...] = idx_acc.astype(jnp.int32)
    gate_out[...] = gate_acc
    rank_out[...] = rank_acc.astype(jnp.int32)


def _outproj(attn_p, attn_s, gm_p, gm_s, w_out_bf, xp, xs, mod4, row_fn, norm2, wr_pad, br_pad):
    ntp = xp.shape[0] // TM
    t = xp.shape[0] + xs.shape[0]
    const = lambda i: (0, 0)
    row = lambda w: pl.BlockSpec((TM, w), lambda i: (i, 0))
    row_p = lambda w: pl.BlockSpec((TM, w), lambda i: (jnp.minimum(i, ntp - 1), 0))
    row_s = lambda w: pl.BlockSpec((TM, w), lambda i: (jnp.maximum(i - ntp, 0), 0))
    return pl.pallas_call(
        functools.partial(_outproj_kernel, prompt_tiles=ntp),
        out_shape=[jax.ShapeDtypeStruct((t, D_MODEL), F32),
                   jax.ShapeDtypeStruct((t, D_MODEL), F32),
                   jax.ShapeDtypeStruct((t, LANES), jnp.int32),
                   jax.ShapeDtypeStruct((t, LANES), F32),
                   jax.ShapeDtypeStruct((t, LANES), jnp.int32),
                   jax.ShapeDtypeStruct((1, LANES), F32)],
        grid=(t // TM,),
        in_specs=[row_p(ATTN_W), row_s(ATTN_W), row_p(GMLP_W), row_s(GMLP_W),
                  pl.BlockSpec((D_MODEL, D_MODEL), const, pipeline_mode=pl.Buffered(1)),
                  row_p(D_MODEL), row_s(D_MODEL),
                  _mod_spec(2, row_fn),
                  pl.BlockSpec((1, D_MODEL), const),
                  _mod_spec(4, row_fn), _mod_spec(3, row_fn),
                  pl.BlockSpec((D_MODEL, LANES), const),
                  pl.BlockSpec((1, LANES), const)],
        out_specs=[row(D_MODEL),
                   row(D_MODEL),
                   row(LANES), row(LANES), row(LANES),
                   pl.BlockSpec((1, LANES), const)],
        scratch_shapes=[pltpu.VMEM((1, LANES), F32)],
        compiler_params=pltpu.CompilerParams(dimension_semantics=("arbitrary",),
                                             vmem_limit_bytes=48 * MIB),
        name="outproj_router",
    )(attn_p, attn_s, gm_p, gm_s, w_out_bf, xp, xs, mod4, norm2, mod4, mod4, wr_pad, br_pad)


def _sort_rows_kernel(tok_ref, nblk_ref, h2_hbm, o_ref, stage, sem):
    i = pl.program_id(0)
    nblk = nblk_ref[0]

    def issue(b):
        slot = b % GSLOTS
        for j in range(SB):
            pltpu.make_async_copy(h2_hbm.at[pl.ds(tok_ref[b * SB + j], 1), :],
                                  stage.at[slot, pl.ds(j, 1), :], sem.at[slot]).start()

    @pl.when(i == 0)
    def _():
        for b in range(GLAG):
            pl.when(b < nblk)(functools.partial(issue, b))

    pl.when(i + GLAG < nblk)(lambda: issue(i + GLAG))

    @pl.when(i < nblk)
    def _():
        slot = i % GSLOTS
        pltpu.make_async_copy(h2_hbm.at[pl.ds(0, SB), :], stage.at[slot], sem.at[slot]).wait()
        o_ref[...] = stage[slot].astype(BF16)

    @pl.when(i >= nblk)
    def _():
        o_ref[...] = jnp.zeros_like(o_ref)


def _sort_rows(tok_sorted, nblk, h2, n_pad):
    return pl.pallas_call(
        _sort_rows_kernel,
        out_shape=jax.ShapeDtypeStruct((n_pad, D_MODEL), BF16),
        grid_spec=pltpu.PrefetchScalarGridSpec(
            num_scalar_prefetch=2,
            grid=(n_pad // SB,),
            in_specs=[pl.BlockSpec(memory_space=pl.ANY)],
            out_specs=pl.BlockSpec((SB, D_MODEL), lambda i, *_: (i, 0)),
            scratch_shapes=[pltpu.VMEM((GSLOTS, SB, D_MODEL), F32),
                            pltpu.SemaphoreType.DMA((GSLOTS,))]),
        compiler_params=pltpu.CompilerParams(dimension_semantics=("arbitrary",),
                                             vmem_limit_bytes=32 * MIB),
        name="sort_rows",
    )(tok_sorted, nblk, h2)


def _moe_kernel(ie_ref, irow_ref, insb_ref, nblk_ref,
                x_hbm, wg_hbm, wu_hbm, wd_hbm, bgu_ref, bd_ref,
                y_hbm,
                x2d, act_buf, wdb, wgus, wds, wgub, ystage, sem_x, sem_w, sem_y, *, n_sub):
    g = pl.program_id(0)
    nsb = insb_ref[g]
    e = ie_ref[g]
    row0 = irow_ref[g]
    nsb_next = insb_ref[g + 1]
    row0_next = irow_ref[g + 1]
    e_next = ie_ref[g + 1]
    xs_cur = g % 2

    def x_copy(src_row, sb, slot):
        rows = pl.ds(_aligned(sb * SB, SB), SB)
        return pltpu.make_async_copy(x_hbm.at[pl.ds(_aligned(src_row + sb * SB, SB), SB), :],
                                     x2d.at[slot, rows, :], sem_x.at[slot])

    def w_copies(ex, f, slot):
        cols = pl.ds(_aligned(f * TF, TF), TF)
        return (pltpu.make_async_copy(wg_hbm.at[ex, :, cols], wgus.at[slot, 0], sem_w.at[slot, 0]),
                pltpu.make_async_copy(wu_hbm.at[ex, :, cols], wgus.at[slot, 1], sem_w.at[slot, 1]),
                pltpu.make_async_copy(wd_hbm.at[ex, cols, :], wds.at[slot], sem_w.at[slot, 2]))

    def y_copy(sb, slot):
        return pltpu.make_async_copy(ystage.at[slot],
                                     y_hbm.at[pl.ds(_aligned(row0 + sb * SB, SB), SB), :], sem_y.at[slot])

    def fetch_next(sb, carry):
        x_copy(row0_next, sb, 1 - xs_cur).start()
        return carry
    lax.fori_loop(0, nsb_next, fetch_next, 0)

    @pl.when(nsb > 0)
    def _():
        def wait_x(sb, carry):
            x_copy(row0, sb, xs_cur).wait()
            return carry
        lax.fori_loop(0, nsb, wait_x, 0)

        @pl.when(g == 1)
        def _():
            for cp in w_copies(e, 0, 0):
                cp.start()

        def gate_up_tile(f, carry):
            slot = f % 2

            @pl.when(f + 1 < NF)
            def _():
                for cp in w_copies(e, f + 1, 1 - slot):
                    cp.start()
            for cp in w_copies(e, f, slot):
                cp.wait()
            wgub[:, 0:TF] = wgus[slot, 0].astype(BF16)
            wgub[:, TF:2 * TF] = wgus[slot, 1].astype(BF16)
            wdb[pl.ds(_aligned(f * TF, TF), TF), :] = wds[slot].astype(BF16)
            b_gu = bgu_ref[pl.ds(e * NF + f, 1), :]

            def gate_up(row_start, m):
                rows = pl.ds(row_start, m)
                gu = jnp.dot(x2d[xs_cur, rows, :], wgub[...], preferred_element_type=F32) + b_gu
                gt = jnp.minimum(gu[:, 0:TF], SWIGLU_LIMIT)
                up = jnp.clip(gu[:, TF:2 * TF], -SWIGLU_LIMIT, SWIGLU_LIMIT)
                act = (up + 1.0) * (gt * jax.nn.sigmoid(SWIGLU_ALPHA * gt))
                act_buf[f, rows, :] = act.astype(BF16)

            for big in BIG_NSB:
                pl.when(nsb == big)(functools.partial(gate_up, 0, big * SB))

            @pl.when(functools.reduce(lambda a, b: a & b, [nsb != big for big in BIG_NSB]))
            def _():
                def one(sb, c2):
                    gate_up(_aligned(sb * SB, SB), SB)
                    return c2
                lax.fori_loop(0, nsb, one, 0)
            return carry
        lax.fori_loop(0, NF, gate_up_tile, 0)

        @pl.when(nsb_next > 0)
        def _():
            for cp in w_copies(e_next, 0, 0):
                cp.start()

        def down(sb, carry):
            yslot = sb % 2

            @pl.when(sb >= 2)
            def _():
                y_copy(sb - 2, yslot).wait()
            rows = pl.ds(_aligned(sb * SB, SB), SB)
            a = jnp.concatenate([act_buf[f, rows, :] for f in range(NF)], axis=1)
            ystage[yslot] = jnp.dot(a, wdb[...], preferred_element_type=F32) + bd_ref[pl.ds(e, 1), :]
            y_copy(sb, yslot).start()
            return carry
        lax.fori_loop(0, nsb, down, 0)
        y_copy(nsb - 1, (nsb - 1) % 2).wait()

        @pl.when(nsb >= 2)
        def _():
            y_copy(nsb - 2, nsb % 2).wait()

    @pl.when(g == pl.num_programs(0) - 1)
    def _():
        ystage[0] = jnp.zeros((SB, D_MODEL), F32)

        def fill(b):
            return pltpu.make_async_copy(ystage.at[0], y_hbm.at[pl.ds(_aligned(b * SB, SB), SB), :],
                                         sem_y.at[0])

        def start_fill(b, carry):
            fill(b).start()
            return carry

        def wait_fill(b, carry):
            fill(b).wait()
            return carry
        lax.fori_loop(nblk_ref[0], n_sub, start_fill, 0)
        lax.fori_loop(nblk_ref[0], n_sub, wait_fill, 0)


def _moe(item_e, item_row, item_nsb, nblk, x_sorted, w_gate, w_up, w_down, b_gate, b_up, b_down,
         n_items):
    any_spec = pl.BlockSpec(memory_space=pl.ANY)
    b_gu = jnp.concatenate([b_gate.reshape(N_EXPERTS * NF, TF), b_up.reshape(N_EXPERTS * NF, TF)], axis=1)
    grid_spec = pltpu.PrefetchScalarGridSpec(
        num_scalar_prefetch=4,
        grid=(n_items + 1,),
        in_specs=[any_spec, any_spec, any_spec, any_spec,
                  pl.BlockSpec((N_EXPERTS * NF, 2 * TF), lambda i, *_: (0, 0)),
                  pl.BlockSpec((N_EXPERTS, D_MODEL), lambda i, *_: (0, 0))],
        out_specs=any_spec,
        scratch_shapes=[pltpu.VMEM((2, RB, D_MODEL), BF16),
                        pltpu.VMEM((NF, RB, TF), BF16),
                        pltpu.VMEM((D_FF, D_MODEL), BF16),
                        pltpu.VMEM((2, 2, D_MODEL, TF), F32),
                        pltpu.VMEM((2, TF, D_MODEL), F32),
                        pltpu.VMEM((D_MODEL, 2 * TF), BF16),
                        pltpu.VMEM((2, SB, D_MODEL), F32),
                        pltpu.SemaphoreType.DMA((2,)),
                        pltpu.SemaphoreType.DMA((2, 3)),
                        pltpu.SemaphoreType.DMA((2,))],
    )
    return pl.pallas_call(
        functools.partial(_moe_kernel, n_sub=x_sorted.shape[0] // SB),
        out_shape=jax.ShapeDtypeStruct(x_sorted.shape, F32),
        grid_spec=grid_spec,
        compiler_params=pltpu.CompilerParams(dimension_semantics=("arbitrary",),
                                             vmem_limit_bytes=56 * MIB),
        name="moe_experts",
    )(item_e, item_row, item_nsb, nblk, x_sorted, w_gate, w_up, w_down, b_gu, b_down)


def _combine_kernel(dest_ref, y_hbm, gate_ref, x1_ref, g2_ref, nf_ref, o_ref, ybuf, sem, *, tok_off):
    i = pl.program_id(0)
    n = pl.num_programs(0)

    def issue(b):
        slot = b % GSLOTS
        for j in range(TC):
            for k in range(TOP_K):
                r = dest_ref[(tok_off + b * TC + j) * TOP_K + k]
                pltpu.make_async_copy(y_hbm.at[pl.ds(r, 1), :], ybuf.at[slot, k, pl.ds(j, 1), :],
                                      sem.at[slot]).start()

    @pl.when(i == 0)
    def _():
        for b in range(GLAG):
            issue(b)

    pl.when(i + GLAG < n)(lambda: issue(i + GLAG))

    slot = i % GSLOTS
    for k in range(TOP_K):
        pltpu.make_async_copy(y_hbm.at[pl.ds(0, TC), :], ybuf.at[slot, k], sem.at[slot]).wait()
    gate = gate_ref[...]
    f = gate[:, 0:1] * ybuf[slot, 0]
    for k in range(1, TOP_K):
        f = f + gate[:, k:k + 1] * ybuf[slot, k]
    x2 = x1_ref[...] + g2_ref[...] * f
    o_ref[...] = _rms(x2) * nf_ref[...]


def _combine(dest, y_sorted, gates, x1, mod4, row_fn, norm_f, tok_off, t):
    off = tok_off // TC
    row_in = lambda w: pl.BlockSpec((TC, w), lambda i, *_: (off + i, 0))
    return pl.pallas_call(
        functools.partial(_combine_kernel, tok_off=tok_off),
        out_shape=jax.ShapeDtypeStruct((t, D_MODEL), F32),
        grid_spec=pltpu.PrefetchScalarGridSpec(
            num_scalar_prefetch=1,
            grid=(t // TC,),
            in_specs=[pl.BlockSpec(memory_space=pl.ANY), row_in(LANES), row_in(D_MODEL),
                      pl.BlockSpec((None, None, 1, D_MODEL),
                                   lambda i, *_: (row_fn(i * TC // TM), 5, 0, 0)),
                      pl.BlockSpec((1, D_MODEL), lambda i, *_: (0, 0))],
            out_specs=pl.BlockSpec((TC, D_MODEL), lambda i, *_: (i, 0)),
            scratch_shapes=[pltpu.VMEM((GSLOTS, TOP_K, TC, D_MODEL), F32),
                            pltpu.SemaphoreType.DMA((GSLOTS,))]),
        compiler_params=pltpu.CompilerParams(dimension_semantics=("arbitrary",),
                                             vmem_limit_bytes=40 * MIB),
        name="combine_norm",
    )(dest, y_sorted, gates, x1, mod4, norm_f)


def _rope_tables(n_tokens):
    n_rows = n_tokens // GRID_W
    row = jnp.repeat(jnp.arange(n_rows, dtype=F32), GRID_W)
    col = jnp.tile(jnp.arange(GRID_W, dtype=F32), n_rows)
    half = AXIS_DIM // 2
    inv_freq = ROPE_THETA ** (-jnp.arange(half, dtype=F32) / half)
    ang_r = row[:, None] * inv_freq[None, :]
    ang_c = col[:, None] * inv_freq[None, :]
    cos_t = jnp.concatenate([jnp.cos(ang_r)] * 2 + [jnp.cos(ang_c)] * 2, axis=-1)
    sin_t = jnp.concatenate([-jnp.sin(ang_r), jnp.sin(ang_r), -jnp.sin(ang_c), jnp.sin(ang_c)], axis=-1)
    return cos_t, sin_t


def kernel(x_prompt, x_sample, cache_k, cache_v, c, c_ctx, w_mod, b_mod, norm1, w_in, q_gain, k_gain,
           w_sp, b_sp, attn_out_gain, gmlp_out_gain, w_out, norm2, w_router, b_router,
           w_gate, b_gate, w_up, b_up, w_down, b_down, norm_f):
    bp, sp_len, _ = x_prompt.shape
    bs, ss_len, _ = x_sample.shape
    depth = w_mod.shape[0]
    n_p = bp * sp_len
    n_s = bs * ss_len
    t_total = n_p + n_s
    n_assign = t_total * TOP_K
    n_sub = (n_assign + N_EXPERTS * (SB - 1)) // SB + 1
    n_pad = n_sub * SB
    n_items = N_EXPERTS + -(-n_pad // RB)

    xp = x_prompt.reshape(n_p, D_MODEL)
    xs = x_sample.reshape(n_s, D_MODEL)
    tiles_per_sample = ss_len // TM
    row_p = lambda i: 0
    row_s = lambda i: 1 + i // tiles_per_sample
    row_all = lambda i: jnp.where(i < n_p // TM, 0, 1 + (i - n_p // TM) // tiles_per_sample)
    rope_tabs = _rope_tables(ss_len)
    cvec8 = jnp.zeros((8, D_MODEL), F32).at[0].set(c_ctx).at[1:1 + bs].set(c)
    new_k, new_v = [], []

    for l in range(depth):
        mod4 = _modulation(cvec8, w_mod[l], b_mod[l]).reshape(8, N_MOD, 1, D_MODEL)
        w_in_bf = w_in[l].astype(BF16)
        w_out_bf = w_out[l].astype(BF16)
        wsp_bf = w_sp[l].astype(BF16)
        bsp_b = jnp.broadcast_to(b_sp[l][:, :, None], (G_HEADS, CHUNK, G_DIM))
        n1 = norm1[l].reshape(1, D_MODEL)
        n2 = norm2[l].reshape(1, D_MODEL)
        qg = q_gain[l].reshape(1, HEAD_DIM)
        kg = k_gain[l].reshape(1, HEAD_DIM)
        gog = gmlp_out_gain[l].reshape(1, GMLP_W)
        aog = attn_out_gain[l].reshape(1, ATTN_W)
        wr_pad = jnp.zeros((D_MODEL, LANES), F32).at[:, :N_EXPERTS].set(w_router[l])
        br_pad = jnp.full((1, LANES), NEG, F32).at[0, :N_EXPERTS].set(b_router[l])

        qp, kp, vp, gmp, kpf, vpf = _inproj(xp, mod4, row_p, n1, w_in_bf, qg, kg, None, wsp_bf, bsp_b,
                                            gog, sp_len, True)
        qs, ks, vs, gms = _inproj(xs, mod4, row_s, n1, w_in_bf, qg, kg, rope_tabs, wsp_bf, bsp_b,
                                  gog, ss_len, False)
        new_k.append(kpf.reshape(bp, sp_len, KV_HEADS, HEAD_DIM))
        new_v.append(vpf.reshape(bp, sp_len, KV_HEADS, HEAD_DIM))

        attn_p = _attention(qp, kp.reshape(bp, sp_len, KV_W), vp.reshape(bp, sp_len, KV_W), aog, sp_len)
        k_all = jnp.concatenate([cache_k[:, l].reshape(bs, -1, KV_W).astype(BF16),
                                 ks.reshape(bs, ss_len, KV_W)], axis=1)
        v_all = jnp.concatenate([cache_v[:, l].reshape(bs, -1, KV_W).astype(BF16),
                                 vs.reshape(bs, ss_len, KV_W)], axis=1)
        attn_s = _attention(qs, k_all, v_all, aog, ss_len)

        x1, h2, idx_l, gates, rank_l, cnt = _outproj(attn_p, attn_s, gmp, gms, w_out_bf, xp, xs, mod4,
                                                     row_all, n2, wr_pad, br_pad)

        idx = idx_l[:, :TOP_K]
        rank = rank_l[:, :TOP_K]
        counts = cnt[0, :N_EXPERTS].astype(jnp.int32)
        padded = ((counts + SB - 1) // SB) * SB
        pad_end = jnp.cumsum(padded)
        pad_start = pad_end - padded
        dest = (pad_start[idx] + rank).astype(jnp.int32)
        tok_of = jnp.broadcast_to(jnp.arange(t_total, dtype=jnp.int32)[:, None], (t_total, TOP_K))
        tok_sorted = jnp.zeros((n_pad,), jnp.int32).at[dest.reshape(-1)].set(tok_of.reshape(-1))
        items_e = (padded + RB - 1) // RB
        items_end = jnp.cumsum(items_e)
        j = jnp.arange(n_items, dtype=jnp.int32)
        item_e = jnp.minimum(jnp.sum((items_end[None, :] <= j[:, None]).astype(jnp.int32), axis=1),
                             N_EXPERTS - 1)
        within = j - (items_end - items_e)[item_e]
        item_row = pad_start[item_e] + within * RB
        item_rows = jnp.clip(padded[item_e] - within * RB, 0, RB)
        item_nsb = jnp.where(j < items_end[-1], item_rows // SB, 0).astype(jnp.int32)
        item_row = jnp.where(item_nsb > 0, item_row, 0).astype(jnp.int32)

        nblk = (pad_end[-1:] // SB).astype(jnp.int32)
        x_sorted = _sort_rows(tok_sorted, nblk, h2, n_pad)
        framed = lambda a: jnp.pad(a.astype(jnp.int32), (1, 1))
        y_sorted = _moe(framed(item_e), framed(item_row), framed(item_nsb), nblk, x_sorted,
                        w_gate[l], w_up[l], w_down[l], b_gate[l], b_up[l], b_down[l], n_items)

        nf = norm_f.reshape(1, D_MODEL)
        is_last = l == depth - 1
        assert is_last, "only DEPTH == 1 is supported"
        dest_flat = dest.reshape(-1)
        y_prompt = _combine(dest_flat, y_sorted, gates, x1, mod4, row_p, nf, 0, n_p)
        y_sample = _combine(dest_flat, y_sorted, gates, x1, mod4, row_s, nf, n_p, n_s)

    return (y_prompt.reshape(x_prompt.shape), y_sample.reshape(x_sample.shape),
            jnp.stack(new_k, axis=1), jnp.stack(new_v, axis=1))
```

```python
import functools

import jax
import jax.numpy as jnp
from jax import lax
from jax.experimental import pallas as pl
from jax.experimental.pallas import tpu as pltpu

F32 = jnp.float32
BF16 = jnp.bfloat16

D_MODEL = 2048
GRID_W = 64
N_HEADS = 8
KV_HEADS = 2
HEAD_DIM = 128
GQA_GROUP = N_HEADS // KV_HEADS
ATTN_W = N_HEADS * HEAD_DIM
KV_W = KV_HEADS * HEAD_DIM
G_HEADS = 8
G_DIM = (D_MODEL - ATTN_W) // G_HEADS
GMLP_W = G_HEADS * G_DIM
IN_W = ATTN_W + 2 * KV_W + 2 * GMLP_W
CHUNK = 128
ROPE_THETA = 10000.0
AXIS_DIM = HEAD_DIM // 2
N_EXPERTS = 32
TOP_K = 4
D_FF = D_MODEL
SWIGLU_ALPHA = 1.702
SWIGLU_LIMIT = 7.0
N_MOD = 6
EPS = 1e-6

LANES = 128
MIB = 1024 * 1024

TM = 256
TQ = 128
TC = 128
MOD_TN = 1024
SB = 256
RB = 1792
TF = 256
NF = D_FF // TF
BIG_NSB = (RB // SB, RB // SB - 1)
GLAG = 2
GSLOTS = GLAG + 1
NEG = -1e30


def _rms(x):
    return x * lax.rsqrt(jnp.mean(x * x, axis=-1, keepdims=True) + EPS)


def _aligned(x, m):
    if isinstance(x, int):
        assert x % m == 0
        return x
    return pl.multiple_of(x, m)


def _mod_kernel(c_ref, w_ref, b_ref, o_ref):
    c = c_ref[...]
    a = c * jax.nn.sigmoid(c)
    o_ref[...] = jnp.dot(a, w_ref[...], preferred_element_type=F32,
                         precision=lax.Precision.HIGHEST) + b_ref[...]


def _modulation(cvec8, w_mod, b_mod):
    n = w_mod.shape[1]
    return pl.pallas_call(
        _mod_kernel,
        out_shape=jax.ShapeDtypeStruct((8, n), F32),
        grid=(n // MOD_TN,),
        in_specs=[pl.BlockSpec((8, D_MODEL), lambda j: (0, 0)),
                  pl.BlockSpec((D_MODEL, MOD_TN), lambda j: (0, j)),
                  pl.BlockSpec((1, MOD_TN), lambda j: (0, j))],
        out_specs=pl.BlockSpec((8, MOD_TN), lambda j: (0, j)),
        compiler_params=pltpu.CompilerParams(dimension_semantics=("arbitrary",),
                                             vmem_limit_bytes=40 * MIB),
        name="modulation",
    )(cvec8, w_mod, b_mod.reshape(1, n))


def _mod_spec(part, row_fn):
    return pl.BlockSpec((None, None, 1, D_MODEL), lambda i, *_: (row_fn(i), part, 0, 0))


def _inproj_kernel(*refs, rope, emit_kv):
    it = iter(refs)
    x_ref, shift_ref, scale_ref, norm_ref, w_ref, qg_ref, kg_ref = (next(it) for _ in range(7))
    cos_ref = sin_ref = None
    if rope:
        cos_ref, sin_ref = next(it), next(it)
    wsp_ref, bsp_ref, gog_ref = next(it), next(it), next(it)
    q_out, k_out, v_out, gm_out = (next(it) for _ in range(4))
    kf_out = vf_out = None
    if emit_kv:
        kf_out, vf_out = next(it), next(it)
    gm_scr = next(it)

    x = x_ref[...]
    h = _rms(x) * (norm_ref[...] * (1.0 + scale_ref[...])) + shift_ref[...]
    z = jnp.dot(h.astype(BF16), w_ref[...], preferred_element_type=F32)

    if rope:
        cosv = cos_ref[...]
        sinv = sin_ref[...]
        lane = lax.broadcasted_iota(jnp.int32, (TM, HEAD_DIM), 1)
        first_half = (lane % AXIS_DIM) < (AXIS_DIM // 2)

    def head(zc, gain):
        n = _rms(zc) * gain
        if rope:
            partner = jnp.where(first_half, pltpu.roll(n, HEAD_DIM - AXIS_DIM // 2, 1),
                                pltpu.roll(n, AXIS_DIM // 2, 1))
            n = n * cosv + partner * sinv
        return n

    qg = qg_ref[...]
    kg = kg_ref[...]
    sm_scale = HEAD_DIM ** -0.5
    for hh in range(N_HEADS):
        sl = slice(hh * HEAD_DIM, (hh + 1) * HEAD_DIM)
        q_out[:, sl] = (head(z[:, sl], qg) * sm_scale).astype(BF16)
    for hh in range(KV_HEADS):
        sl = slice(hh * HEAD_DIM, (hh + 1) * HEAD_DIM)
        kh = head(z[:, ATTN_W + hh * HEAD_DIM:ATTN_W + (hh + 1) * HEAD_DIM], kg)
        k_out[:, sl] = kh.astype(BF16)
        if emit_kv:
            kf_out[:, sl] = kh
    v = z[:, ATTN_W + KV_W:ATTN_W + 2 * KV_W]
    v_out[...] = v.astype(BF16)
    if emit_kv:
        vf_out[...] = v

    u0 = ATTN_W + 2 * KV_W
    g0 = u0 + GMLP_W
    for hh in range(G_HEADS):
        sl = slice(hh * G_DIM, (hh + 1) * G_DIM)
        u = jax.nn.gelu(z[:, u0 + hh * G_DIM:u0 + (hh + 1) * G_DIM])
        g = _rms(jax.nn.gelu(z[:, g0 + hh * G_DIM:g0 + (hh + 1) * G_DIM])).astype(BF16)
        w_h = wsp_ref[hh]
        b_h = bsp_ref[hh]
        for c in range(TM // CHUNK):
            rs = slice(c * CHUNK, (c + 1) * CHUNK)
            sp = jnp.dot(w_h, g[rs], preferred_element_type=F32) + b_h
            gm_scr[rs, sl] = u[rs] * sp
    gm_out[...] = (_rms(gm_scr[...]) * gog_ref[...]).astype(BF16)


def _inproj(x2d, mod4, row_fn, norm1, w_in_bf, q_gain, k_gain, rope_tabs, wsp_bf, bsp_b, gog,
            seq_len, emit_kv):
    t = x2d.shape[0]
    rope = rope_tabs is not None
    tiles_per_seq = seq_len // TM
    const = lambda i: (0, 0)
    in_specs = [pl.BlockSpec((TM, D_MODEL), lambda i: (i, 0)),
                _mod_spec(0, row_fn), _mod_spec(1, row_fn),
                pl.BlockSpec((1, D_MODEL), const),
                pl.BlockSpec((D_MODEL, IN_W), const, pipeline_mode=pl.Buffered(1)),
                pl.BlockSpec((1, HEAD_DIM), const),
                pl.BlockSpec((1, HEAD_DIM), const)]
    args = [x2d, mod4, mod4, norm1, w_in_bf, q_gain, k_gain]
    if rope:
        tab_spec = pl.BlockSpec((TM, HEAD_DIM), lambda i: (i % tiles_per_seq, 0))
        in_specs += [tab_spec, tab_spec]
        args += list(rope_tabs)
    in_specs += [pl.BlockSpec((G_HEADS, CHUNK, CHUNK), lambda i: (0, 0, 0)),
                 pl.BlockSpec((G_HEADS, CHUNK, G_DIM), lambda i: (0, 0, 0)),
                 pl.BlockSpec((1, GMLP_W), const)]
    args += [wsp_bf, bsp_b, gog]
    row = lambda w: pl.BlockSpec((TM, w), lambda i: (i, 0))
    out_shape = [jax.ShapeDtypeStruct((t, ATTN_W), BF16), jax.ShapeDtypeStruct((t, KV_W), BF16),
                 jax.ShapeDtypeStruct((t, KV_W), BF16), jax.ShapeDtypeStruct((t, GMLP_W), BF16)]
    out_specs = [row(ATTN_W), row(KV_W), row(KV_W), row(GMLP_W)]
    if emit_kv:
        out_shape += [jax.ShapeDtypeStruct((t, KV_W), F32)] * 2
        out_specs += [row(KV_W), row(KV_W)]
    return pl.pallas_call(
        functools.partial(_inproj_kernel, rope=rope, emit_kv=emit_kv),
        out_shape=out_shape,
        grid=(t // TM,),
        in_specs=in_specs,
        out_specs=out_specs,
        scratch_shapes=[pltpu.VMEM((TM, GMLP_W), F32)],
        compiler_params=pltpu.CompilerParams(dimension_semantics=("arbitrary",),
                                             vmem_limit_bytes=48 * MIB),
        name="inproj_rope" if rope else "inproj",
    )(*args)


def _attn_kernel(q_ref, k_ref, v_ref, gain_ref, o_ref):
    outs = [None] * N_HEADS
    for kvh in range(KV_HEADS):
        ks = slice(kvh * HEAD_DIM, (kvh + 1) * HEAD_DIM)
        kh = k_ref[:, ks]
        vh = v_ref[:, ks]
        q4 = jnp.concatenate(
            [q_ref[:, (kvh * GQA_GROUP + g) * HEAD_DIM:(kvh * GQA_GROUP + g + 1) * HEAD_DIM]
             for g in range(GQA_GROUP)], axis=0)
        s = lax.dot_general(q4, kh, (((1,), (1,)), ((), ())), preferred_element_type=F32)
        m = jnp.max(s, axis=-1, keepdims=True)
        p = jnp.exp(s - m)
        l = jnp.sum(p, axis=-1, keepdims=True)
        o = jnp.dot(p.astype(BF16), vh, preferred_element_type=F32) / l
        for g in range(GQA_GROUP):
            outs[kvh * GQA_GROUP + g] = o[g * TQ:(g + 1) * TQ]
    ss = outs[0] * outs[0]
    for hh in range(1, N_HEADS):
        ss = ss + outs[hh] * outs[hh]
    inv = lax.rsqrt(jnp.sum(ss, axis=-1, keepdims=True) * (1.0 / ATTN_W) + EPS)
    gain = gain_ref[...]
    for hh in range(N_HEADS):
        sl = slice(hh * HEAD_DIM, (hh + 1) * HEAD_DIM)
        o_ref[:, sl] = (outs[hh] * inv * gain[:, sl]).astype(BF16)


def _attention(q2d, k3d, v3d, gain, seq_len):
    t = q2d.shape[0]
    b, l, _ = k3d.shape
    nq = seq_len // TQ
    return pl.pallas_call(
        _attn_kernel,
        out_shape=jax.ShapeDtypeStruct((t, ATTN_W), BF16),
        grid=(b, nq),
        in_specs=[pl.BlockSpec((TQ, ATTN_W), lambda bi, i: (bi * nq + i, 0)),
                  pl.BlockSpec((None, l, KV_W), lambda bi, i: (bi, 0, 0)),
                  pl.BlockSpec((None, l, KV_W), lambda bi, i: (bi, 0, 0)),
                  pl.BlockSpec((1, ATTN_W), lambda bi, i: (0, 0))],
        out_specs=pl.BlockSpec((TQ, ATTN_W), lambda bi, i: (bi * nq + i, 0)),
        compiler_params=pltpu.CompilerParams(dimension_semantics=("arbitrary", "arbitrary"),
                                             vmem_limit_bytes=48 * MIB),
        name="attention",
    )(q2d, k3d, v3d, gain)


def _outproj_kernel(ap_ref, as_ref, gmp_ref, gms_ref, w_ref, xp_ref, xs_ref, g1_ref, n2_ref, sc2_ref,
                    sh2_ref, wr_ref, br_ref, x1_out, h2_out, idx_out, gate_out, rank_out, cnt_out,
                    cnt_scr, *, prompt_tiles):
    i = pl.program_id(0)
    is_p = i < prompt_tiles

    @pl.when(i == 0)
    def _():
        cnt_scr[...] = jnp.zeros_like(cnt_scr)

    a = jnp.where(is_p, ap_ref[...], as_ref[...])
    gm = jnp.where(is_p, gmp_ref[...], gms_ref[...])
    x = jnp.where(is_p, xp_ref[...], xs_ref[...])
    o = (jnp.dot(a, w_ref[0:ATTN_W, :], preferred_element_type=F32)
         + jnp.dot(gm, w_ref[ATTN_W:, :], preferred_element_type=F32))
    x1 = x + g1_ref[...] * o
    x1_out[...] = x1
    h2 = _rms(x1) * (n2_ref[...] * (1.0 + sc2_ref[...])) + sh2_ref[...]
    h2_out[...] = h2

    h_hi = h2.astype(BF16)
    h_lo = (h2 - h_hi.astype(F32)).astype(BF16)
    wr = wr_ref[...]
    w_hi = wr.astype(BF16)
    w_lo = (wr - w_hi.astype(F32)).astype(BF16)
    logits = (jnp.dot(h_hi, w_hi, preferred_element_type=F32)
              + jnp.dot(h_lo, w_hi, preferred_element_type=F32)
              + jnp.dot(h_hi, w_lo, preferred_element_type=F32)) + br_ref[...]

    lane = lax.broadcasted_iota(jnp.int32, (TM, LANES), 1)
    lane_f = lane.astype(F32)
    work = logits
    vals, sels = [], []
    multi = jnp.zeros((TM, LANES), F32)
    idx_acc = jnp.zeros((TM, LANES), F32)
    for k in range(TOP_K):
        m = jnp.max(work, axis=-1, keepdims=True)
        idx = jnp.min(jnp.where(work == m, lane_f, float(LANES)), axis=-1, keepdims=True)
        sel = lane_f == idx
        vals.append(m)
        sels.append(sel)
        work = jnp.where(sel, -jnp.inf, work)
        multi = multi + sel.astype(F32)
        idx_acc = jnp.where(lane == k, idx, idx_acc)
    es = [jnp.exp(v - vals[0]) for v in vals]
    den = es[0] + es[1] + es[2] + es[3]
    gate_acc = jnp.zeros((TM, LANES), F32)
    for k in range(TOP_K):
        gate_acc = jnp.where(lane == k, es[k] / den, gate_acc)

    r_i = lax.broadcasted_iota(jnp.int32, (TM, TM), 0)
    c_i = lax.broadcasted_iota(jnp.int32, (TM, TM), 1)
    lower = (c_i < r_i).astype(BF16)
    prefix = jnp.dot(lower, multi.astype(BF16), preferred_element_type=F32) + cnt_scr[...]
    rank_acc = jnp.zeros((TM, LANES), F32)
    for k in range(TOP_K):
        r = jnp.sum(jnp.where(sels[k], prefix, 0.0), axis=-1, keepdims=True)
        rank_acc = jnp.where(lane == k, r, rank_acc)
    cnt_new = cnt_scr[...] + jnp.sum(multi, axis=0, keepdims=True)
    cnt_scr[...] = cnt_new
    cnt_out[...] = cnt_new
    idx_out[...] = idx_acc.astype(jnp.int32)
    gate_out[...] = gate_acc
    rank_out[...] = rank_acc.astype(jnp.int32)


def _outproj(attn_p, attn_s, gm_p, gm_s, w_out_bf, xp, xs, mod4, row_fn, norm2, wr_pad, br_pad):
    ntp = xp.shape[0] // TM
    t = xp.shape[0] + xs.shape[0]
    const = lambda i: (0, 0)
    row = lambda w: pl.BlockSpec((TM, w), lambda i: (i, 0))
    row_p = lambda w: pl.BlockSpec((TM, w), lambda i: (jnp.minimum(i, ntp - 1), 0))
    row_s = lambda w: pl.BlockSpec((TM, w), lambda i: (jnp.maximum(i - ntp, 0), 0))
    return pl.pallas_call(
        functools.partial(_outproj_kernel, prompt_tiles=ntp),
        out_shape=[jax.ShapeDtypeStruct((t, D_MODEL), F32),
                   jax.ShapeDtypeStruct((t, D_MODEL), F32),
                   jax.ShapeDtypeStruct((t, LANES), jnp.int32),
                   jax.ShapeDtypeStruct((t, LANES), F32),
                   jax.ShapeDtypeStruct((t, LANES), jnp.int32),
                   jax.ShapeDtypeStruct((1, LANES), F32)],
        grid=(t // TM,),
        in_specs=[row_p(ATTN_W), row_s(ATTN_W), row_p(GMLP_W), row_s(GMLP_W),
                  pl.BlockSpec((D_MODEL, D_MODEL), const, pipeline_mode=pl.Buffered(1)),
                  row_p(D_MODEL), row_s(D_MODEL),
                  _mod_spec(2, row_fn),
                  pl.BlockSpec((1, D_MODEL), const),
                  _mod_spec(4, row_fn), _mod_spec(3, row_fn),
                  pl.BlockSpec((D_MODEL, LANES), const),
                  pl.BlockSpec((1, LANES), const)],
        out_specs=[row(D_MODEL),
                   row(D_MODEL),
                   row(LANES), row(LANES), row(LANES),
                   pl.BlockSpec((1, LANES), const)],
        scratch_shapes=[pltpu.VMEM((1, LANES), F32)],
        compiler_params=pltpu.CompilerParams(dimension_semantics=("arbitrary",),
                                             vmem_limit_bytes=48 * MIB),
        name="outproj_router",
    )(attn_p, attn_s, gm_p, gm_s, w_out_bf, xp, xs, mod4, norm2, mod4, mod4, wr_pad, br_pad)


def _scatter_rows_kernel(dest_ref, cend_ref, pend_ref, nblk_ref, h_hbm, x_hbm, ring, zbuf,
                         sem_in, sem_sc, sem_z, *, n_sub):
    i = pl.program_id(0)
    n = pl.num_programs(0)

    def in_copy(b):
        return pltpu.make_async_copy(h_hbm.at[pl.ds(_aligned(b * TC, TC), TC), :], ring.at[b % GSLOTS],
                                     sem_in.at[b % GSLOTS])

    def sc_wait(b):
        for _ in range(TOP_K):
            pltpu.make_async_copy(ring.at[b % GSLOTS], x_hbm.at[pl.ds(0, TC), :],
                                  sem_sc.at[b % GSLOTS]).wait()

    def zero_row(r):
        return pltpu.make_async_copy(zbuf.at[pl.ds(0, 1), :], x_hbm.at[pl.ds(r, 1), :], sem_z)

    def zero_block(b):
        return pltpu.make_async_copy(zbuf, x_hbm.at[pl.ds(_aligned(b * SB, SB), SB), :], sem_z)

    @pl.when(i == 0)
    def _():
        for b in range(GLAG):
            in_copy(b).start()
        zbuf[...] = jnp.zeros_like(zbuf)

        def per_expert(ex, carry):
            def start_row(r, c2):
                zero_row(r).start()
                return c2
            lax.fori_loop(cend_ref[ex], pend_ref[ex], start_row, 0)
            return carry
        lax.fori_loop(0, N_EXPERTS, per_expert, 0)

        def start_blk(b, carry):
            zero_block(b).start()
            return carry
        lax.fori_loop(nblk_ref[0], n_sub, start_blk, 0)

    in_copy(i).wait()
    slot = i % GSLOTS
    for j in range(TC):
        for k in range(TOP_K):
            r = dest_ref[(i * TC + j) * TOP_K + k]
            pltpu.make_async_copy(ring.at[slot, pl.ds(j, 1), :], x_hbm.at[pl.ds(r, 1), :],
                                  sem_sc.at[slot]).start()

    @pl.when(i >= 1)
    def _():
        sc_wait(i - 1)

    @pl.when(i + GLAG < n)
    def _():
        in_copy(i + GLAG).start()

    @pl.when(i == n - 1)
    def _():
        sc_wait(i)

        def per_expert(ex, carry):
            def wait_row(r, c2):
                zero_row(0).wait()
                return c2
            lax.fori_loop(cend_ref[ex], pend_ref[ex], wait_row, 0)
            return carry
        lax.fori_loop(0, N_EXPERTS, per_expert, 0)

        def wait_blk(b, carry):
            zero_block(0).wait()
            return carry
        lax.fori_loop(nblk_ref[0], n_sub, wait_blk, 0)


def _scatter_rows(dest_flat, cnt_end, pad_end, nblk, h_words, n_pad):
    t = h_words.shape[0]
    any_spec = pl.BlockSpec(memory_space=pl.ANY)
    return pl.pallas_call(
        functools.partial(_scatter_rows_kernel, n_sub=n_pad // SB),
        out_shape=jax.ShapeDtypeStruct((n_pad, D_MODEL), F32),
        grid_spec=pltpu.PrefetchScalarGridSpec(
            num_scalar_prefetch=4,
            grid=(t // TC,),
            in_specs=[any_spec],
            out_specs=any_spec,
            scratch_shapes=[pltpu.VMEM((GSLOTS, TC, D_MODEL), F32),
                            pltpu.VMEM((SB, D_MODEL), F32),
                            pltpu.SemaphoreType.DMA((GSLOTS,)),
                            pltpu.SemaphoreType.DMA((GSLOTS,)),
                            pltpu.SemaphoreType.DMA]),
        compiler_params=pltpu.CompilerParams(dimension_semantics=("arbitrary",),
                                             vmem_limit_bytes=32 * MIB),
        name="scatter_rows",
    )(dest_flat, cnt_end, pad_end, nblk, h_words)


def _moe_kernel(ie_ref, irow_ref, insb_ref, nblk_ref,
                x_hbm, wg_hbm, wu_hbm, wd_hbm, bgu_ref, bd_ref,
                y_hbm,
                xring, x2d, act_buf, wdb, wgus, wds, wgub, ystage, sem_x, sem_w, sem_y, *, n_sub):
    g = pl.program_id(0)
    nsb = insb_ref[g]
    e = ie_ref[g]
    row0 = irow_ref[g]
    nsb_next = insb_ref[g + 1]
    row0_next = irow_ref[g + 1]
    e_next = ie_ref[g + 1]

    nsb_prev = insb_ref[jnp.maximum(g - 1, 0)]

    def x_copy(src_row, c):
        return pltpu.make_async_copy(x_hbm.at[pl.ds(_aligned(src_row + c * SB, SB), SB), :],
                                     xring.at[c % 2], sem_x.at[c % 2])

    def land(src_row, c, n_total):
        x_copy(src_row, c).wait()
        x2d[pl.ds(_aligned(c * SB, SB), SB), :] = xring[c % 2].astype(BF16)

        @pl.when(c + 2 < n_total)
        def _():
            x_copy(src_row, c + 2).start()

    def issue_first(src_row, n_total):
        for c in range(2):
            pl.when(c < n_total)(lambda c=c: x_copy(src_row, c).start())

    def w_copies(ex, f, slot):
        cols = pl.ds(_aligned(f * TF, TF), TF)
        return (pltpu.make_async_copy(wg_hbm.at[ex, :, cols], wgus.at[slot, 0], sem_w.at[slot, 0]),
                pltpu.make_async_copy(wu_hbm.at[ex, :, cols], wgus.at[slot, 1], sem_w.at[slot, 1]),
                pltpu.make_async_copy(wd_hbm.at[ex, cols, :], wds.at[slot], sem_w.at[slot, 2]))

    def y_copy(sb, slot):
        return pltpu.make_async_copy(ystage.at[slot],
                                     y_hbm.at[pl.ds(_aligned(row0 + sb * SB, SB), SB), :], sem_y.at[slot])

    def catch_up(c, carry):
        land(row0, c, nsb)
        return carry
    lax.fori_loop(jnp.minimum(nsb_prev, nsb), nsb, catch_up, 0)

    @pl.when(nsb == 0)
    def _():
        issue_first(row0_next, nsb_next)

    @pl.when(nsb > 0)
    def _():

        @pl.when(g == 1)
        def _():
            for cp in w_copies(e, 0, 0):
                cp.start()

        def gate_up_tile(f, carry):
            slot = f % 2

            @pl.when(f + 1 < NF)
            def _():
                for cp in w_copies(e, f + 1, 1 - slot):
                    cp.start()
            for cp in w_copies(e, f, slot):
                cp.wait()
            wgub[:, 0:TF] = wgus[slot, 0].astype(BF16)
            wgub[:, TF:2 * TF] = wgus[slot, 1].astype(BF16)
            wdb[pl.ds(_aligned(f * TF, TF), TF), :] = wds[slot].astype(BF16)
            b_gu = bgu_ref[pl.ds(e * NF + f, 1), :]

            def gate_up(row_start, m):
                rows = pl.ds(row_start, m)
                gu = jnp.dot(x2d[rows, :], wgub[...], preferred_element_type=F32) + b_gu
                gt = jnp.minimum(gu[:, 0:TF], SWIGLU_LIMIT)
                up = jnp.clip(gu[:, TF:2 * TF], -SWIGLU_LIMIT, SWIGLU_LIMIT)
                act = (up + 1.0) * (gt * jax.nn.sigmoid(SWIGLU_ALPHA * gt))
                act_buf[f, rows, :] = act.astype(BF16)

            for big in BIG_NSB:
                pl.when(nsb == big)(functools.partial(gate_up, 0, big * SB))

            @pl.when(functools.reduce(lambda a, b: a & b, [nsb != big for big in BIG_NSB]))
            def _():
                def one(sb, c2):
                    gate_up(_aligned(sb * SB, SB), SB)
                    return c2
                lax.fori_loop(0, nsb, one, 0)
            return carry
        lax.fori_loop(0, NF, gate_up_tile, 0)

        @pl.when(nsb_next > 0)
        def _():
            for cp in w_copies(e_next, 0, 0):
                cp.start()

        issue_first(row0_next, nsb_next)

        def down(sb, carry):
            yslot = sb % 2

            @pl.when(sb >= 2)
            def _():
                y_copy(sb - 2, yslot).wait()

            @pl.when(sb < nsb_next)
            def _():
                land(row0_next, sb, nsb_next)
            rows = pl.ds(_aligned(sb * SB, SB), SB)
            a = jnp.concatenate([act_buf[f, rows, :] for f in range(NF)], axis=1)
            ystage[yslot] = jnp.dot(a, wdb[...], preferred_element_type=F32) + bd_ref[pl.ds(e, 1), :]
            y_copy(sb, yslot).start()
            return carry
        lax.fori_loop(0, nsb, down, 0)
        y_copy(nsb - 1, (nsb - 1) % 2).wait()

        @pl.when(nsb >= 2)
        def _():
            y_copy(nsb - 2, nsb % 2).wait()

    @pl.when(g == pl.num_programs(0) - 1)
    def _():
        ystage[0] = jnp.zeros((SB, D_MODEL), F32)

        def fill(b):
            return pltpu.make_async_copy(ystage.at[0], y_hbm.at[pl.ds(_aligned(b * SB, SB), SB), :],
                                         sem_y.at[0])

        def start_fill(b, carry):
            fill(b).start()
            return carry

        def wait_fill(b, carry):
            fill(b).wait()
            return carry
        lax.fori_loop(nblk_ref[0], n_sub, start_fill, 0)
        lax.fori_loop(nblk_ref[0], n_sub, wait_fill, 0)


def _moe(item_e, item_row, item_nsb, nblk, x_sorted, w_gate, w_up, w_down, b_gate, b_up, b_down,
         n_items):
    any_spec = pl.BlockSpec(memory_space=pl.ANY)
    b_gu = jnp.concatenate([b_gate.reshape(N_EXPERTS * NF, TF), b_up.reshape(N_EXPERTS * NF, TF)], axis=1)
    grid_spec = pltpu.PrefetchScalarGridSpec(
        num_scalar_prefetch=4,
        grid=(n_items + 1,),
        in_specs=[any_spec, any_spec, any_spec, any_spec,
                  pl.BlockSpec((N_EXPERTS * NF, 2 * TF), lambda i, *_: (0, 0)),
                  pl.BlockSpec((N_EXPERTS, D_MODEL), lambda i, *_: (0, 0))],
        out_specs=any_spec,
        scratch_shapes=[pltpu.VMEM((2, SB, D_MODEL), F32),
                        pltpu.VMEM((RB, D_MODEL), BF16),
                        pltpu.VMEM((NF, RB, TF), BF16),
                        pltpu.VMEM((D_FF, D_MODEL), BF16),
                        pltpu.VMEM((2, 2, D_MODEL, TF), F32),
                        pltpu.VMEM((2, TF, D_MODEL), F32),
                        pltpu.VMEM((D_MODEL, 2 * TF), BF16),
                        pltpu.VMEM((2, SB, D_MODEL), F32),
                        pltpu.SemaphoreType.DMA((2,)),
                        pltpu.SemaphoreType.DMA((2, 3)),
                        pltpu.SemaphoreType.DMA((2,))],
    )
    return pl.pallas_call(
        functools.partial(_moe_kernel, n_sub=x_sorted.shape[0] // SB),
        out_shape=jax.ShapeDtypeStruct((x_sorted.shape[0], D_MODEL), F32),
        grid_spec=grid_spec,
        compiler_params=pltpu.CompilerParams(dimension_semantics=("arbitrary",),
                                             vmem_limit_bytes=56 * MIB),
        name="moe_experts",
    )(item_e, item_row, item_nsb, nblk, x_sorted, w_gate, w_up, w_down, b_gu, b_down)


def _combine_kernel(dest_ref, y_hbm, gate_ref, x1_ref, g2_ref, nf_ref, o_ref, ybuf, sem, *, tok_off):
    i = pl.program_id(0)
    n = pl.num_programs(0)

    def issue(b):
        slot = b % GSLOTS
        for j in range(TC):
            for k in range(TOP_K):
                r = dest_ref[(tok_off + b * TC + j) * TOP_K + k]
                pltpu.make_async_copy(y_hbm.at[pl.ds(r, 1), :], ybuf.at[slot, k, pl.ds(j, 1), :],
                                      sem.at[slot]).start()

    @pl.when(i == 0)
    def _():
        for b in range(GLAG):
            issue(b)

    pl.when(i + GLAG < n)(lambda: issue(i + GLAG))

    slot = i % GSLOTS
    for k in range(TOP_K):
        pltpu.make_async_copy(y_hbm.at[pl.ds(0, TC), :], ybuf.at[slot, k], sem.at[slot]).wait()
    gate = gate_ref[...]
    f = gate[:, 0:1] * ybuf[slot, 0]
    for k in range(1, TOP_K):
        f = f + gate[:, k:k + 1] * ybuf[slot, k]
    x2 = x1_ref[...] + g2_ref[...] * f
    o_ref[...] = _rms(x2) * nf_ref[...]


def _combine(dest, y_sorted, gates, x1, mod4, row_fn, norm_f, tok_off, t):
    off = tok_off // TC
    row_in = lambda w: pl.BlockSpec((TC, w), lambda i, *_: (off + i, 0))
    return pl.pallas_call(
        functools.partial(_combine_kernel, tok_off=tok_off),
        out_shape=jax.ShapeDtypeStruct((t, D_MODEL), F32),
        grid_spec=pltpu.PrefetchScalarGridSpec(
            num_scalar_prefetch=1,
            grid=(t // TC,),
            in_specs=[pl.BlockSpec(memory_space=pl.ANY), row_in(LANES), row_in(D_MODEL),
                      pl.BlockSpec((None, None, 1, D_MODEL),
                                   lambda i, *_: (row_fn(i * TC // TM), 5, 0, 0)),
                      pl.BlockSpec((1, D_MODEL), lambda i, *_: (0, 0))],
            out_specs=pl.BlockSpec((TC, D_MODEL), lambda i, *_: (i, 0)),
            scratch_shapes=[pltpu.VMEM((GSLOTS, TOP_K, TC, D_MODEL), F32),
                            pltpu.SemaphoreType.DMA((GSLOTS,))]),
        compiler_params=pltpu.CompilerParams(dimension_semantics=("arbitrary",),
                                             vmem_limit_bytes=40 * MIB),
        name="combine_norm",
    )(dest, y_sorted, gates, x1, mod4, norm_f)


def _rope_tables(n_tokens):
    n_rows = n_tokens // GRID_W
    row = jnp.repeat(jnp.arange(n_rows, dtype=F32), GRID_W)
    col = jnp.tile(jnp.arange(GRID_W, dtype=F32), n_rows)
    half = AXIS_DIM // 2
    inv_freq = ROPE_THETA ** (-jnp.arange(half, dtype=F32) / half)
    ang_r = row[:, None] * inv_freq[None, :]
    ang_c = col[:, None] * inv_freq[None, :]
    cos_t = jnp.concatenate([jnp.cos(ang_r)] * 2 + [jnp.cos(ang_c)] * 2, axis=-1)
    sin_t = jnp.concatenate([-jnp.sin(ang_r), jnp.sin(ang_r), -jnp.sin(ang_c), jnp.sin(ang_c)], axis=-1)
    return cos_t, sin_t


def kernel(x_prompt, x_sample, cache_k, cache_v, c, c_ctx, w_mod, b_mod, norm1, w_in, q_gain, k_gain,
           w_sp, b_sp, attn_out_gain, gmlp_out_gain, w_out, norm2, w_router, b_router,
           w_gate, b_gate, w_up, b_up, w_down, b_down, norm_f):
    bp, sp_len, _ = x_prompt.shape
    bs, ss_len, _ = x_sample.shape
    depth = w_mod.shape[0]
    n_p = bp * sp_len
    n_s = bs * ss_len
    t_total = n_p + n_s
    n_assign = t_total * TOP_K
    n_sub = (n_assign + N_EXPERTS * (SB - 1)) // SB + 1
    n_pad = n_sub * SB
    n_items = N_EXPERTS + -(-n_pad // RB)

    xp = x_prompt.reshape(n_p, D_MODEL)
    xs = x_sample.reshape(n_s, D_MODEL)
    tiles_per_sample = ss_len // TM
    row_p = lambda i: 0
    row_s = lambda i: 1 + i // tiles_per_sample
    row_all = lambda i: jnp.where(i < n_p // TM, 0, 1 + (i - n_p // TM) // tiles_per_sample)
    rope_tabs = _rope_tables(ss_len)
    cvec8 = jnp.zeros((8, D_MODEL), F32).at[0].set(c_ctx).at[1:1 + bs].set(c)
    new_k, new_v = [], []

    for l in range(depth):
        mod4 = _modulation(cvec8, w_mod[l], b_mod[l]).reshape(8, N_MOD, 1, D_MODEL)
        w_in_bf = w_in[l].astype(BF16)
        w_out_bf = w_out[l].astype(BF16)
        wsp_bf = w_sp[l].astype(BF16)
        bsp_b = jnp.broadcast_to(b_sp[l][:, :, None], (G_HEADS, CHUNK, G_DIM))
        n1 = norm1[l].reshape(1, D_MODEL)
        n2 = norm2[l].reshape(1, D_MODEL)
        qg = q_gain[l].reshape(1, HEAD_DIM)
        kg = k_gain[l].reshape(1, HEAD_DIM)
        gog = gmlp_out_gain[l].reshape(1, GMLP_W)
        aog = attn_out_gain[l].reshape(1, ATTN_W)
        wr_pad = jnp.zeros((D_MODEL, LANES), F32).at[:, :N_EXPERTS].set(w_router[l])
        br_pad = jnp.full((1, LANES), NEG, F32).at[0, :N_EXPERTS].set(b_router[l])

        qp, kp, vp, gmp, kpf, vpf = _inproj(xp, mod4, row_p, n1, w_in_bf, qg, kg, None, wsp_bf, bsp_b,
                                            gog, sp_len, True)
        qs, ks, vs, gms = _inproj(xs, mod4, row_s, n1, w_in_bf, qg, kg, rope_tabs, wsp_bf, bsp_b,
                                  gog, ss_len, False)
        new_k.append(kpf.reshape(bp, sp_len, KV_HEADS, HEAD_DIM))
        new_v.append(vpf.reshape(bp, sp_len, KV_HEADS, HEAD_DIM))

        attn_p = _attention(qp, kp.reshape(bp, sp_len, KV_W), vp.reshape(bp, sp_len, KV_W), aog, sp_len)
        k_all = jnp.concatenate([cache_k[:, l].reshape(bs, -1, KV_W).astype(BF16),
                                 ks.reshape(bs, ss_len, KV_W)], axis=1)
        v_all = jnp.concatenate([cache_v[:, l].reshape(bs, -1, KV_W).astype(BF16),
                                 vs.reshape(bs, ss_len, KV_W)], axis=1)
        attn_s = _attention(qs, k_all, v_all, aog, ss_len)

        x1, h2, idx_l, gates, rank_l, cnt = _outproj(attn_p, attn_s, gmp, gms, w_out_bf, xp, xs, mod4,
                                                     row_all, n2, wr_pad, br_pad)

        idx = idx_l[:, :TOP_K]
        rank = rank_l[:, :TOP_K]
        counts = cnt[0, :N_EXPERTS].astype(jnp.int32)
        padded = ((counts + SB - 1) // SB) * SB
        pad_end = jnp.cumsum(padded)
        pad_start = pad_end - padded
        dest = (pad_start[idx] + rank).astype(jnp.int32)
        dest_flat = dest.reshape(-1)
        items_e = (padded + RB - 1) // RB
        items_end = jnp.cumsum(items_e)
        j = jnp.arange(n_items, dtype=jnp.int32)
        item_e = jnp.minimum(jnp.sum((items_end[None, :] <= j[:, None]).astype(jnp.int32), axis=1),
                             N_EXPERTS - 1)
        within = j - (items_end - items_e)[item_e]
        item_row = pad_start[item_e] + within * RB
        item_rows = jnp.clip(padded[item_e] - within * RB, 0, RB)
        item_nsb = jnp.where(j < items_end[-1], item_rows // SB, 0).astype(jnp.int32)
        item_row = jnp.where(item_nsb > 0, item_row, 0).astype(jnp.int32)

        nblk = (pad_end[-1:] // SB).astype(jnp.int32)
        x_sorted = _scatter_rows(dest_flat, (pad_start + counts).astype(jnp.int32),
                                 pad_end.astype(jnp.int32), nblk, h2, n_pad)
        framed = lambda a: jnp.pad(a.astype(jnp.int32), (1, 1))
        y_sorted = _moe(framed(item_e), framed(item_row), framed(item_nsb), nblk, x_sorted,
                        w_gate[l], w_up[l], w_down[l], b_gate[l], b_up[l], b_down[l], n_items)

        nf = norm_f.reshape(1, D_MODEL)
        is_last = l == depth - 1
        assert is_last, "only DEPTH == 1 is supported"
        dest_flat = dest.reshape(-1)
        y_prompt = _combine(dest_flat, y_sorted, gates, x1, mod4, row_p, nf, 0, n_p)
        y_sample = _combine(dest_flat, y_sorted, gates, x1, mod4, row_s, nf, n_p, n_s)

    return (y_prompt.reshape(x_prompt.shape), y_sample.reshape(x_sample.shape),
            jnp.stack(new_k, axis=1), jnp.stack(new_v, axis=1))
```

```python
import functools

import jax
import jax.numpy as jnp
from jax import lax
from jax.experimental import pallas as pl
from jax.experimental.pallas import tpu as pltpu

F32 = jnp.float32
BF16 = jnp.bfloat16

D_MODEL = 2048
GRID_W = 64
N_HEADS = 8
KV_HEADS = 2
HEAD_DIM = 128
GQA_GROUP = N_HEADS // KV_HEADS
ATTN_W = N_HEADS * HEAD_DIM
KV_W = KV_HEADS * HEAD_DIM
G_HEADS = 8
G_DIM = (D_MODEL - ATTN_W) // G_HEADS
GMLP_W = G_HEADS * G_DIM
IN_W = ATTN_W + 2 * KV_W + 2 * GMLP_W
CHUNK = 128
ROPE_THETA = 10000.0
AXIS_DIM = HEAD_DIM // 2
N_EXPERTS = 32
TOP_K = 4
D_FF = D_MODEL
SWIGLU_ALPHA = 1.702
SWIGLU_LIMIT = 7.0
N_MOD = 6
EPS = 1e-6

LANES = 128
MIB = 1024 * 1024

TM = 256
TQ = 256
KCHUNK = 1152
TC = 128
MOD_TN = 1024
SB = 256
RB = 1792
TF = 256
NF = D_FF // TF
BIG_NSB = (RB // SB, RB // SB - 1)
GLAG = 2
GSLOTS = GLAG + 1
NEG = -1e30


def _rms(x):
    return x * lax.rsqrt(jnp.mean(x * x, axis=-1, keepdims=True) + EPS)


def _aligned(x, m):
    if isinstance(x, int):
        assert x % m == 0
        return x
    return pl.multiple_of(x, m)


def _mod_kernel(c_ref, w_ref, b_ref, o_ref):
    c = c_ref[...]
    a = c * jax.nn.sigmoid(c)
    o_ref[...] = jnp.dot(a, w_ref[...], preferred_element_type=F32,
                         precision=lax.Precision.HIGHEST) + b_ref[...]


def _modulation(cvec8, w_mod, b_mod):
    n = w_mod.shape[1]
    return pl.pallas_call(
        _mod_kernel,
        out_shape=jax.ShapeDtypeStruct((8, n), F32),
        grid=(n // MOD_TN,),
        in_specs=[pl.BlockSpec((8, D_MODEL), lambda j: (0, 0)),
                  pl.BlockSpec((D_MODEL, MOD_TN), lambda j: (0, j)),
                  pl.BlockSpec((1, MOD_TN), lambda j: (0, j))],
        out_specs=pl.BlockSpec((8, MOD_TN), lambda j: (0, j)),
        compiler_params=pltpu.CompilerParams(dimension_semantics=("arbitrary",),
                                             vmem_limit_bytes=40 * MIB),
        name="modulation",
    )(cvec8, w_mod, b_mod.reshape(1, n))


def _mod_spec(part, row_fn):
    return pl.BlockSpec((None, None, 1, D_MODEL), lambda i, *_: (row_fn(i), part, 0, 0))


def _inproj_kernel(*refs, rope, emit_kv):
    it = iter(refs)
    x_ref, shift_ref, scale_ref, norm_ref, w_ref, qg_ref, kg_ref = (next(it) for _ in range(7))
    cos_ref = sin_ref = None
    if rope:
        cos_ref, sin_ref = next(it), next(it)
    wsp_ref, bsp_ref, gog_ref = next(it), next(it), next(it)
    q_out, k_out, v_out, gm_out = (next(it) for _ in range(4))
    kf_out = vf_out = None
    if emit_kv:
        kf_out, vf_out = next(it), next(it)
    gm_scr = next(it)

    x = x_ref[...]
    h = _rms(x) * (norm_ref[...] * (1.0 + scale_ref[...])) + shift_ref[...]
    z = jnp.dot(h.astype(BF16), w_ref[...], preferred_element_type=F32)

    if rope:
        cosv = cos_ref[...]
        sinv = sin_ref[...]
        lane = lax.broadcasted_iota(jnp.int32, (TM, HEAD_DIM), 1)
        first_half = (lane % AXIS_DIM) < (AXIS_DIM // 2)

    def head(zc, gain):
        n = _rms(zc) * gain
        if rope:
            partner = jnp.where(first_half, pltpu.roll(n, HEAD_DIM - AXIS_DIM // 2, 1),
                                pltpu.roll(n, AXIS_DIM // 2, 1))
            n = n * cosv + partner * sinv
        return n

    qg = qg_ref[...]
    kg = kg_ref[...]
    sm_scale = HEAD_DIM ** -0.5
    for hh in range(N_HEADS):
        sl = slice(hh * HEAD_DIM, (hh + 1) * HEAD_DIM)
        q_out[:, sl] = (head(z[:, sl], qg) * sm_scale).astype(BF16)
    for hh in range(KV_HEADS):
        sl = slice(hh * HEAD_DIM, (hh + 1) * HEAD_DIM)
        kh = head(z[:, ATTN_W + hh * HEAD_DIM:ATTN_W + (hh + 1) * HEAD_DIM], kg)
        k_out[:, sl] = kh.astype(BF16)
        if emit_kv:
            kf_out[:, sl] = kh
    v = z[:, ATTN_W + KV_W:ATTN_W + 2 * KV_W]
    v_out[...] = v.astype(BF16)
    if emit_kv:
        vf_out[...] = v

    u0 = ATTN_W + 2 * KV_W
    g0 = u0 + GMLP_W
    for hh in range(G_HEADS):
        sl = slice(hh * G_DIM, (hh + 1) * G_DIM)
        u = jax.nn.gelu(z[:, u0 + hh * G_DIM:u0 + (hh + 1) * G_DIM])
        g = _rms(jax.nn.gelu(z[:, g0 + hh * G_DIM:g0 + (hh + 1) * G_DIM])).astype(BF16)
        w_h = wsp_ref[hh]
        b_h = bsp_ref[hh]
        for c in range(TM // CHUNK):
            rs = slice(c * CHUNK, (c + 1) * CHUNK)
            sp = jnp.dot(w_h, g[rs], preferred_element_type=F32) + b_h
            gm_scr[rs, sl] = u[rs] * sp
    gm_out[...] = (_rms(gm_scr[...]) * gog_ref[...]).astype(BF16)


def _inproj(x2d, mod4, row_fn, norm1, w_in_bf, q_gain, k_gain, rope_tabs, wsp_bf, bsp_b, gog,
            seq_len, emit_kv):
    t = x2d.shape[0]
    rope = rope_tabs is not None
    tiles_per_seq = seq_len // TM
    const = lambda i: (0, 0)
    in_specs = [pl.BlockSpec((TM, D_MODEL), lambda i: (i, 0)),
                _mod_spec(0, row_fn), _mod_spec(1, row_fn),
                pl.BlockSpec((1, D_MODEL), const),
                pl.BlockSpec((D_MODEL, IN_W), const, pipeline_mode=pl.Buffered(1)),
                pl.BlockSpec((1, HEAD_DIM), const),
                pl.BlockSpec((1, HEAD_DIM), const)]
    args = [x2d, mod4, mod4, norm1, w_in_bf, q_gain, k_gain]
    if rope:
        tab_spec = pl.BlockSpec((TM, HEAD_DIM), lambda i: (i % tiles_per_seq, 0))
        in_specs += [tab_spec, tab_spec]
        args += list(rope_tabs)
    in_specs += [pl.BlockSpec((G_HEADS, CHUNK, CHUNK), lambda i: (0, 0, 0)),
                 pl.BlockSpec((G_HEADS, CHUNK, G_DIM), lambda i: (0, 0, 0)),
                 pl.BlockSpec((1, GMLP_W), const)]
    args += [wsp_bf, bsp_b, gog]
    row = lambda w: pl.BlockSpec((TM, w), lambda i: (i, 0))
    out_shape = [jax.ShapeDtypeStruct((t, ATTN_W), BF16), jax.ShapeDtypeStruct((t, KV_W), BF16),
                 jax.ShapeDtypeStruct((t, KV_W), BF16), jax.ShapeDtypeStruct((t, GMLP_W), BF16)]
    out_specs = [row(ATTN_W), row(KV_W), row(KV_W), row(GMLP_W)]
    if emit_kv:
        out_shape += [jax.ShapeDtypeStruct((t, KV_W), F32)] * 2
        out_specs += [row(KV_W), row(KV_W)]
    return pl.pallas_call(
        functools.partial(_inproj_kernel, rope=rope, emit_kv=emit_kv),
        out_shape=out_shape,
        grid=(t // TM,),
        in_specs=in_specs,
        out_specs=out_specs,
        scratch_shapes=[pltpu.VMEM((TM, GMLP_W), F32)],
        compiler_params=pltpu.CompilerParams(dimension_semantics=("arbitrary",),
                                             vmem_limit_bytes=48 * MIB),
        name="inproj_rope" if rope else "inproj",
    )(*args)


def _attn_kernel(q_ref, k_ref, v_ref, gain_ref, o_ref, *, chunks):
    outs = [None] * N_HEADS
    for kvh in range(KV_HEADS):
        ks = slice(kvh * HEAD_DIM, (kvh + 1) * HEAD_DIM)
        q4 = jnp.concatenate(
            [q_ref[:, (kvh * GQA_GROUP + g) * HEAD_DIM:(kvh * GQA_GROUP + g + 1) * HEAD_DIM]
             for g in range(GQA_GROUP)], axis=0)
        m = l = acc = None
        for c0, cl in chunks:
            s = lax.dot_general(q4, k_ref[c0:c0 + cl, ks], (((1,), (1,)), ((), ())),
                                preferred_element_type=F32)
            mc = jnp.max(s, axis=-1, keepdims=True)
            m_new = mc if m is None else jnp.maximum(m, mc)
            p = jnp.exp(s - m_new)
            ps = jnp.sum(p, axis=-1, keepdims=True)
            pv = jnp.dot(p.astype(BF16), v_ref[c0:c0 + cl, ks], preferred_element_type=F32)
            if m is None:
                l, acc = ps, pv
            else:
                corr = jnp.exp(m - m_new)
                l = l * corr + ps
                acc = acc * corr + pv
            m = m_new
        o = acc / l
        for g in range(GQA_GROUP):
            outs[kvh * GQA_GROUP + g] = o[g * TQ:(g + 1) * TQ]
    ss = outs[0] * outs[0]
    for hh in range(1, N_HEADS):
        ss = ss + outs[hh] * outs[hh]
    inv = lax.rsqrt(jnp.sum(ss, axis=-1, keepdims=True) * (1.0 / ATTN_W) + EPS)
    gain = gain_ref[...]
    for hh in range(N_HEADS):
        sl = slice(hh * HEAD_DIM, (hh + 1) * HEAD_DIM)
        o_ref[:, sl] = (outs[hh] * inv * gain[:, sl]).astype(BF16)


def _attention(q2d, k3d, v3d, gain, seq_len):
    t = q2d.shape[0]
    b, l, _ = k3d.shape
    nq = seq_len // TQ
    n_chunks = -(-l // KCHUNK)
    base = (l // LANES) // n_chunks
    extra = (l // LANES) % n_chunks
    sizes = [(base + (1 if c < extra else 0)) * LANES for c in range(n_chunks)]
    chunks = tuple((sum(sizes[:c]), sizes[c]) for c in range(n_chunks))
    return pl.pallas_call(
        functools.partial(_attn_kernel, chunks=chunks),
        out_shape=jax.ShapeDtypeStruct((t, ATTN_W), BF16),
        grid=(b, nq),
        in_specs=[pl.BlockSpec((TQ, ATTN_W), lambda bi, i: (bi * nq + i, 0)),
                  pl.BlockSpec((None, l, KV_W), lambda bi, i: (bi, 0, 0)),
                  pl.BlockSpec((None, l, KV_W), lambda bi, i: (bi, 0, 0)),
                  pl.BlockSpec((1, ATTN_W), lambda bi, i: (0, 0))],
        out_specs=pl.BlockSpec((TQ, ATTN_W), lambda bi, i: (bi * nq + i, 0)),
        compiler_params=pltpu.CompilerParams(dimension_semantics=("arbitrary", "arbitrary"),
                                             vmem_limit_bytes=48 * MIB),
        name="attention",
    )(q2d, k3d, v3d, gain)


def _outproj_kernel(ap_ref, as_ref, gmp_ref, gms_ref, w_ref, xp_ref, xs_ref, g1_ref, n2_ref, sc2_ref,
                    sh2_ref, wr_ref, br_ref, x1_out, h2_out, idx_out, gate_out, rank_out, cnt_out,
                    cnt_scr, *, prompt_tiles):
    i = pl.program_id(0)
    is_p = i < prompt_tiles

    @pl.when(i == 0)
    def _():
        cnt_scr[...] = jnp.zeros_like(cnt_scr)

    a = jnp.where(is_p, ap_ref[...], as_ref[...])
    gm = jnp.where(is_p, gmp_ref[...], gms_ref[...])
    x = jnp.where(is_p, xp_ref[...], xs_ref[...])
    o = (jnp.dot(a, w_ref[0:ATTN_W, :], preferred_element_type=F32)
         + jnp.dot(gm, w_ref[ATTN_W:, :], preferred_element_type=F32))
    x1 = x + g1_ref[...] * o
    x1_out[...] = x1
    h2 = _rms(x1) * (n2_ref[...] * (1.0 + sc2_ref[...])) + sh2_ref[...]
    h2_out[...] = h2

    h_hi = h2.astype(BF16)
    h_lo = (h2 - h_hi.astype(F32)).astype(BF16)
    wr = wr_ref[...]
    w_hi = wr.astype(BF16)
    w_lo = (wr - w_hi.astype(F32)).astype(BF16)
    logits = (jnp.dot(h_hi, w_hi, preferred_element_type=F32)
              + jnp.dot(h_lo, w_hi, preferred_element_type=F32)
              + jnp.dot(h_hi, w_lo, preferred_element_type=F32)) + br_ref[...]

    lane = lax.broadcasted_iota(jnp.int32, (TM, LANES), 1)
    lane_f = lane.astype(F32)
    work = logits
    vals, sels = [], []
    multi = jnp.zeros((TM, LANES), F32)
    idx_acc = jnp.zeros((TM, LANES), F32)
    for k in range(TOP_K):
        m = jnp.max(work, axis=-1, keepdims=True)
        idx = jnp.min(jnp.where(work == m, lane_f, float(LANES)), axis=-1, keepdims=True)
        sel = lane_f == idx
        vals.append(m)
        sels.append(sel)
        work = jnp.where(sel, -jnp.inf, work)
        multi = multi + sel.astype(F32)
        idx_acc = jnp.where(lane == k, idx, idx_acc)
    es = [jnp.exp(v - vals[0]) for v in vals]
    den = es[0] + es[1] + es[2] + es[3]
    gate_acc = jnp.zeros((TM, LANES), F32)
    for k in range(TOP_K):
        gate_acc = jnp.where(lane == k, es[k] / den, gate_acc)

    r_i = lax.broadcasted_iota(jnp.int32, (TM, TM), 0)
    c_i = lax.broadcasted_iota(jnp.int32, (TM, TM), 1)
    lower = (c_i < r_i).astype(BF16)
    prefix = jnp.dot(lower, multi.astype(BF16), preferred_element_type=F32) + cnt_scr[...]
    rank_acc = jnp.zeros((TM, LANES), F32)
    for k in range(TOP_K):
        r = jnp.sum(jnp.where(sels[k], prefix, 0.0), axis=-1, keepdims=True)
        rank_acc = jnp.where(lane == k, r, rank_acc)
    cnt_new = cnt_scr[...] + jnp.sum(multi, axis=0, keepdims=True)
    cnt_scr[...] = cnt_new
    cnt_out[...] = cnt_new
    idx_out[...] = idx_acc.astype(jnp.int32)
    gate_out[...] = gate_acc
    rank_out[...] = rank_acc.astype(jnp.int32)


def _outproj(attn_p, attn_s, gm_p, gm_s, w_out_bf, xp, xs, mod4, row_fn, norm2, wr_pad, br_pad):
    ntp = xp.shape[0] // TM
    t = xp.shape[0] + xs.shape[0]
    const = lambda i: (0, 0)
    row = lambda w: pl.BlockSpec((TM, w), lambda i: (i, 0))
    row_p = lambda w: pl.BlockSpec((TM, w), lambda i: (jnp.minimum(i, ntp - 1), 0))
    row_s = lambda w: pl.BlockSpec((TM, w), lambda i: (jnp.maximum(i - ntp, 0), 0))
    return pl.pallas_call(
        functools.partial(_outproj_kernel, prompt_tiles=ntp),
        out_shape=[jax.ShapeDtypeStruct((t, D_MODEL), F32),
                   jax.ShapeDtypeStruct((t, D_MODEL), F32),
                   jax.ShapeDtypeStruct((t, LANES), jnp.int32),
                   jax.ShapeDtypeStruct((t, LANES), F32),
                   jax.ShapeDtypeStruct((t, LANES), jnp.int32),
                   jax.ShapeDtypeStruct((1, LANES), F32)],
        grid=(t // TM,),
        in_specs=[row_p(ATTN_W), row_s(ATTN_W), row_p(GMLP_W), row_s(GMLP_W),
                  pl.BlockSpec((D_MODEL, D_MODEL), const, pipeline_mode=pl.Buffered(1)),
                  row_p(D_MODEL), row_s(D_MODEL),
                  _mod_spec(2, row_fn),
                  pl.BlockSpec((1, D_MODEL), const),
                  _mod_spec(4, row_fn), _mod_spec(3, row_fn),
                  pl.BlockSpec((D_MODEL, LANES), const),
                  pl.BlockSpec((1, LANES), const)],
        out_specs=[row(D_MODEL),
                   row(D_MODEL),
                   row(LANES), row(LANES), row(LANES),
                   pl.BlockSpec((1, LANES), const)],
        scratch_shapes=[pltpu.VMEM((1, LANES), F32)],
        compiler_params=pltpu.CompilerParams(dimension_semantics=("arbitrary",),
                                             vmem_limit_bytes=48 * MIB),
        name="outproj_router",
    )(attn_p, attn_s, gm_p, gm_s, w_out_bf, xp, xs, mod4, norm2, mod4, mod4, wr_pad, br_pad)


def _scatter_rows_kernel(dest_ref, cend_ref, pend_ref, nblk_ref, h_hbm, x_hbm, ring, zbuf,
                         sem_in, sem_sc, sem_z, *, n_sub):
    i = pl.program_id(0)
    n = pl.num_programs(0)

    def in_copy(b):
        return pltpu.make_async_copy(h_hbm.at[pl.ds(_aligned(b * TC, TC), TC), :], ring.at[b % GSLOTS],
                                     sem_in.at[b % GSLOTS])

    def sc_wait(b):
        for _ in range(TOP_K):
            pltpu.make_async_copy(ring.at[b % GSLOTS], x_hbm.at[pl.ds(0, TC), :],
                                  sem_sc.at[b % GSLOTS]).wait()

    def zero_row(r):
        return pltpu.make_async_copy(zbuf.at[pl.ds(0, 1), :], x_hbm.at[pl.ds(r, 1), :], sem_z)

    def zero_block(b):
        return pltpu.make_async_copy(zbuf, x_hbm.at[pl.ds(_aligned(b * SB, SB), SB), :], sem_z)

    @pl.when(i == 0)
    def _():
        for b in range(GLAG):
            in_copy(b).start()
        zbuf[...] = jnp.zeros_like(zbuf)

        def per_expert(ex, carry):
            def start_row(r, c2):
                zero_row(r).start()
                return c2
            lax.fori_loop(cend_ref[ex], pend_ref[ex], start_row, 0)
            return carry
        lax.fori_loop(0, N_EXPERTS, per_expert, 0)

        def start_blk(b, carry):
            zero_block(b).start()
            return carry
        lax.fori_loop(nblk_ref[0], n_sub, start_blk, 0)

    in_copy(i).wait()
    slot = i % GSLOTS
    for j in range(TC):
        for k in range(TOP_K):
            r = dest_ref[(i * TC + j) * TOP_K + k]
            pltpu.make_async_copy(ring.at[slot, pl.ds(j, 1), :], x_hbm.at[pl.ds(r, 1), :],
                                  sem_sc.at[slot]).start()

    @pl.when(i >= 1)
    def _():
        sc_wait(i - 1)

    @pl.when(i + GLAG < n)
    def _():
        in_copy(i + GLAG).start()

    @pl.when(i == n - 1)
    def _():
        sc_wait(i)

        def per_expert(ex, carry):
            def wait_row(r, c2):
                zero_row(0).wait()
                return c2
            lax.fori_loop(cend_ref[ex], pend_ref[ex], wait_row, 0)
            return carry
        lax.fori_loop(0, N_EXPERTS, per_expert, 0)

        def wait_blk(b, carry):
            zero_block(0).wait()
            return carry
        lax.fori_loop(nblk_ref[0], n_sub, wait_blk, 0)


def _scatter_rows(dest_flat, cnt_end, pad_end, nblk, h_words, n_pad):
    t = h_words.shape[0]
    any_spec = pl.BlockSpec(memory_space=pl.ANY)
    return pl.pallas_call(
        functools.partial(_scatter_rows_kernel, n_sub=n_pad // SB),
        out_shape=jax.ShapeDtypeStruct((n_pad, D_MODEL), F32),
        grid_spec=pltpu.PrefetchScalarGridSpec(
            num_scalar_prefetch=4,
            grid=(t // TC,),
            in_specs=[any_spec],
            out_specs=any_spec,
            scratch_shapes=[pltpu.VMEM((GSLOTS, TC, D_MODEL), F32),
                            pltpu.VMEM((SB, D_MODEL), F32),
                            pltpu.SemaphoreType.DMA((GSLOTS,)),
                            pltpu.SemaphoreType.DMA((GSLOTS,)),
                            pltpu.SemaphoreType.DMA]),
        compiler_params=pltpu.CompilerParams(dimension_semantics=("arbitrary",),
                                             vmem_limit_bytes=32 * MIB),
        name="scatter_rows",
    )(dest_flat, cnt_end, pad_end, nblk, h_words)


def _moe_kernel(ie_ref, irow_ref, insb_ref, nblk_ref,
                x_hbm, wg_hbm, wu_hbm, wd_hbm, bgu_ref, bd_ref,
                y_hbm,
                xring, x2d, act_buf, wdb, wgus, wds, wgub, ystage, sem_x, sem_w, sem_y, *, n_sub):
    g = pl.program_id(0)
    nsb = insb_ref[g]
    e = ie_ref[g]
    row0 = irow_ref[g]
    nsb_next = insb_ref[g + 1]
    row0_next = irow_ref[g + 1]
    e_next = ie_ref[g + 1]

    nsb_prev = insb_ref[jnp.maximum(g - 1, 0)]

    def x_copy(src_row, c):
        return pltpu.make_async_copy(x_hbm.at[pl.ds(_aligned(src_row + c * SB, SB), SB), :],
                                     xring.at[c % 2], sem_x.at[c % 2])

    def land(src_row, c, n_total):
        x_copy(src_row, c).wait()
        x2d[pl.ds(_aligned(c * SB, SB), SB), :] = xring[c % 2].astype(BF16)

        @pl.when(c + 2 < n_total)
        def _():
            x_copy(src_row, c + 2).start()

    def issue_first(src_row, n_total):
        for c in range(2):
            pl.when(c < n_total)(lambda c=c: x_copy(src_row, c).start())

    def w_copies(ex, f, slot):
        cols = pl.ds(_aligned(f * TF, TF), TF)
        return (pltpu.make_async_copy(wg_hbm.at[ex, :, cols], wgus.at[slot, 0], sem_w.at[slot, 0]),
                pltpu.make_async_copy(wu_hbm.at[ex, :, cols], wgus.at[slot, 1], sem_w.at[slot, 1]),
                pltpu.make_async_copy(wd_hbm.at[ex, cols, :], wds.at[slot], sem_w.at[slot, 2]))

    def y_copy(sb, slot):
        return pltpu.make_async_copy(ystage.at[slot],
                                     y_hbm.at[pl.ds(_aligned(row0 + sb * SB, SB), SB), :], sem_y.at[slot])

    def catch_up(c, carry):
        land(row0, c, nsb)
        return carry
    lax.fori_loop(jnp.minimum(nsb_prev, nsb), nsb, catch_up, 0)

    @pl.when(nsb == 0)
    def _():
        issue_first(row0_next, nsb_next)

    @pl.when(nsb > 0)
    def _():

        @pl.when(g == 1)
        def _():
            for cp in w_copies(e, 0, 0):
                cp.start()

        def gate_up_tile(f, carry):
            slot = f % 2

            @pl.when(f + 1 < NF)
            def _():
                for cp in w_copies(e, f + 1, 1 - slot):
                    cp.start()
            for cp in w_copies(e, f, slot):
                cp.wait()
            wgub[:, 0:TF] = wgus[slot, 0].astype(BF16)
            wgub[:, TF:2 * TF] = wgus[slot, 1].astype(BF16)
            wdb[pl.ds(_aligned(f * TF, TF), TF), :] = wds[slot].astype(BF16)
            b_gu = bgu_ref[pl.ds(e * NF + f, 1), :]

            def gate_up(row_start, m):
                rows = pl.ds(row_start, m)
                gu = jnp.dot(x2d[rows, :], wgub[...], preferred_element_type=F32) + b_gu
                gt = jnp.minimum(gu[:, 0:TF], SWIGLU_LIMIT)
                up = jnp.clip(gu[:, TF:2 * TF], -SWIGLU_LIMIT, SWIGLU_LIMIT)
                act = (up + 1.0) * (gt * jax.nn.sigmoid(SWIGLU_ALPHA * gt))
                act_buf[f, rows, :] = act.astype(BF16)

            for big in BIG_NSB:
                @pl.when(nsb == big)
                def _(big=big):
                    half = big * SB // 2
                    gate_up(0, half)
                    gate_up(half, half)

            @pl.when(functools.reduce(lambda a, b: a & b, [nsb != big for big in BIG_NSB]))
            def _():
                def one(sb, c2):
                    gate_up(_aligned(sb * SB, SB), SB)
                    return c2
                lax.fori_loop(0, nsb, one, 0)
            return carry
        lax.fori_loop(0, NF, gate_up_tile, 0)

        @pl.when(nsb_next > 0)
        def _():
            for cp in w_copies(e_next, 0, 0):
                cp.start()

        issue_first(row0_next, nsb_next)

        def down(sb, carry):
            yslot = sb % 2

            @pl.when(sb >= 2)
            def _():
                y_copy(sb - 2, yslot).wait()

            @pl.when(sb < nsb_next)
            def _():
                land(row0_next, sb, nsb_next)
            rows = pl.ds(_aligned(sb * SB, SB), SB)
            a = jnp.concatenate([act_buf[f, rows, :] for f in range(NF)], axis=1)
            ystage[yslot] = jnp.dot(a, wdb[...], preferred_element_type=F32) + bd_ref[pl.ds(e, 1), :]
            y_copy(sb, yslot).start()
            return carry
        lax.fori_loop(0, nsb, down, 0)
        y_copy(nsb - 1, (nsb - 1) % 2).wait()

        @pl.when(nsb >= 2)
        def _():
            y_copy(nsb - 2, nsb % 2).wait()

    @pl.when(g == pl.num_programs(0) - 1)
    def _():
        ystage[0] = jnp.zeros((SB, D_MODEL), F32)

        def fill(b):
            return pltpu.make_async_copy(ystage.at[0], y_hbm.at[pl.ds(_aligned(b * SB, SB), SB), :],
                                         sem_y.at[0])

        def start_fill(b, carry):
            fill(b).start()
            return carry

        def wait_fill(b, carry):
            fill(b).wait()
            return carry
        lax.fori_loop(nblk_ref[0], n_sub, start_fill, 0)
        lax.fori_loop(nblk_ref[0], n_sub, wait_fill, 0)


def _moe(item_e, item_row, item_nsb, nblk, x_sorted, w_gate, w_up, w_down, b_gate, b_up, b_down,
         n_items):
    any_spec = pl.BlockSpec(memory_space=pl.ANY)
    b_gu = jnp.concatenate([b_gate.reshape(N_EXPERTS * NF, TF), b_up.reshape(N_EXPERTS * NF, TF)], axis=1)
    grid_spec = pltpu.PrefetchScalarGridSpec(
        num_scalar_prefetch=4,
        grid=(n_items + 1,),
        in_specs=[any_spec, any_spec, any_spec, any_spec,
                  pl.BlockSpec((N_EXPERTS * NF, 2 * TF), lambda i, *_: (0, 0)),
                  pl.BlockSpec((N_EXPERTS, D_MODEL), lambda i, *_: (0, 0))],
        out_specs=any_spec,
        scratch_shapes=[pltpu.VMEM((2, SB, D_MODEL), F32),
                        pltpu.VMEM((RB, D_MODEL), BF16),
                        pltpu.VMEM((NF, RB, TF), BF16),
                        pltpu.VMEM((D_FF, D_MODEL), BF16),
                        pltpu.VMEM((2, 2, D_MODEL, TF), F32),
                        pltpu.VMEM((2, TF, D_MODEL), F32),
                        pltpu.VMEM((D_MODEL, 2 * TF), BF16),
                        pltpu.VMEM((2, SB, D_MODEL), F32),
                        pltpu.SemaphoreType.DMA((2,)),
                        pltpu.SemaphoreType.DMA((2, 3)),
                        pltpu.SemaphoreType.DMA((2,))],
    )
    return pl.pallas_call(
        functools.partial(_moe_kernel, n_sub=x_sorted.shape[0] // SB),
        out_shape=jax.ShapeDtypeStruct((x_sorted.shape[0], D_MODEL), F32),
        grid_spec=grid_spec,
        compiler_params=pltpu.CompilerParams(dimension_semantics=("arbitrary",),
                                             vmem_limit_bytes=56 * MIB),
        name="moe_experts",
    )(item_e, item_row, item_nsb, nblk, x_sorted, w_gate, w_up, w_down, b_gu, b_down)


def _combine_kernel(dest_ref, y_hbm, gate_ref, x1_ref, g2_ref, nf_ref, o_ref, ybuf, sem, *, tok_off):
    i = pl.program_id(0)
    n = pl.num_programs(0)

    def issue(b):
        slot = b % GSLOTS
        for j in range(TC):
            for k in range(TOP_K):
                r = dest_ref[(tok_off + b * TC + j) * TOP_K + k]
                pltpu.make_async_copy(y_hbm.at[pl.ds(r, 1), :], ybuf.at[slot, k, pl.ds(j, 1), :],
                                      sem.at[slot]).start()

    @pl.when(i == 0)
    def _():
        for b in range(GLAG):
            issue(b)

    pl.when(i + GLAG < n)(lambda: issue(i + GLAG))

    slot = i % GSLOTS
    for k in range(TOP_K):
        pltpu.make_async_copy(y_hbm.at[pl.ds(0, TC), :], ybuf.at[slot, k], sem.at[slot]).wait()
    gate = gate_ref[...]
    f = gate[:, 0:1] * ybuf[slot, 0]
    for k in range(1, TOP_K):
        f = f + gate[:, k:k + 1] * ybuf[slot, k]
    x2 = x1_ref[...] + g2_ref[...] * f
    o_ref[...] = _rms(x2) * nf_ref[...]


def _combine(dest, y_sorted, gates, x1, mod4, row_fn, norm_f, tok_off, t):
    off = tok_off // TC
    row_in = lambda w: pl.BlockSpec((TC, w), lambda i, *_: (off + i, 0))
    return pl.pallas_call(
        functools.partial(_combine_kernel, tok_off=tok_off),
        out_shape=jax.ShapeDtypeStruct((t, D_MODEL), F32),
        grid_spec=pltpu.PrefetchScalarGridSpec(
            num_scalar_prefetch=1,
            grid=(t // TC,),
            in_specs=[pl.BlockSpec(memory_space=pl.ANY), row_in(LANES), row_in(D_MODEL),
                      pl.BlockSpec((None, None, 1, D_MODEL),
                                   lambda i, *_: (row_fn(i * TC // TM), 5, 0, 0)),
                      pl.BlockSpec((1, D_MODEL), lambda i, *_: (0, 0))],
            out_specs=pl.BlockSpec((TC, D_MODEL), lambda i, *_: (i, 0)),
            scratch_shapes=[pltpu.VMEM((GSLOTS, TOP_K, TC, D_MODEL), F32),
                            pltpu.SemaphoreType.DMA((GSLOTS,))]),
        compiler_params=pltpu.CompilerParams(dimension_semantics=("arbitrary",),
                                             vmem_limit_bytes=40 * MIB),
        name="combine_norm",
    )(dest, y_sorted, gates, x1, mod4, norm_f)


def _rope_tables(n_tokens):
    n_rows = n_tokens // GRID_W
    row = jnp.repeat(jnp.arange(n_rows, dtype=F32), GRID_W)
    col = jnp.tile(jnp.arange(GRID_W, dtype=F32), n_rows)
    half = AXIS_DIM // 2
    inv_freq = ROPE_THETA ** (-jnp.arange(half, dtype=F32) / half)
    ang_r = row[:, None] * inv_freq[None, :]
    ang_c = col[:, None] * inv_freq[None, :]
    cos_t = jnp.concatenate([jnp.cos(ang_r)] * 2 + [jnp.cos(ang_c)] * 2, axis=-1)
    sin_t = jnp.concatenate([-jnp.sin(ang_r), jnp.sin(ang_r), -jnp.sin(ang_c), jnp.sin(ang_c)], axis=-1)
    return cos_t, sin_t


def kernel(x_prompt, x_sample, cache_k, cache_v, c, c_ctx, w_mod, b_mod, norm1, w_in, q_gain, k_gain,
           w_sp, b_sp, attn_out_gain, gmlp_out_gain, w_out, norm2, w_router, b_router,
           w_gate, b_gate, w_up, b_up, w_down, b_down, norm_f):
    bp, sp_len, _ = x_prompt.shape
    bs, ss_len, _ = x_sample.shape
    depth = w_mod.shape[0]
    n_p = bp * sp_len
    n_s = bs * ss_len
    t_total = n_p + n_s
    n_assign = t_total * TOP_K
    n_sub = (n_assign + N_EXPERTS * (SB - 1)) // SB + 1
    n_pad = n_sub * SB
    n_items = N_EXPERTS + -(-n_pad // RB)

    xp = x_prompt.reshape(n_p, D_MODEL)
    xs = x_sample.reshape(n_s, D_MODEL)
    tiles_per_sample = ss_len // TM
    row_p = lambda i: 0
    row_s = lambda i: 1 + i // tiles_per_sample
    row_all = lambda i: jnp.where(i < n_p // TM, 0, 1 + (i - n_p // TM) // tiles_per_sample)
    rope_tabs = _rope_tables(ss_len)
    cvec8 = jnp.zeros((8, D_MODEL), F32).at[0].set(c_ctx).at[1:1 + bs].set(c)
    new_k, new_v = [], []

    for l in range(depth):
        mod4 = _modulation(cvec8, w_mod[l], b_mod[l]).reshape(8, N_MOD, 1, D_MODEL)
        w_in_bf = w_in[l].astype(BF16)
        w_out_bf = w_out[l].astype(BF16)
        wsp_bf = w_sp[l].astype(BF16)
        bsp_b = jnp.broadcast_to(b_sp[l][:, :, None], (G_HEADS, CHUNK, G_DIM))
        n1 = norm1[l].reshape(1, D_MODEL)
        n2 = norm2[l].reshape(1, D_MODEL)
        qg = q_gain[l].reshape(1, HEAD_DIM)
        kg = k_gain[l].reshape(1, HEAD_DIM)
        gog = gmlp_out_gain[l].reshape(1, GMLP_W)
        aog = attn_out_gain[l].reshape(1, ATTN_W)
        wr_pad = jnp.zeros((D_MODEL, LANES), F32).at[:, :N_EXPERTS].set(w_router[l])
        br_pad = jnp.full((1, LANES), NEG, F32).at[0, :N_EXPERTS].set(b_router[l])

        qp, kp, vp, gmp, kpf, vpf = _inproj(xp, mod4, row_p, n1, w_in_bf, qg, kg, None, wsp_bf, bsp_b,
                                            gog, sp_len, True)
        qs, ks, vs, gms = _inproj(xs, mod4, row_s, n1, w_in_bf, qg, kg, rope_tabs, wsp_bf, bsp_b,
                                  gog, ss_len, False)
        new_k.append(kpf.reshape(bp, sp_len, KV_HEADS, HEAD_DIM))
        new_v.append(vpf.reshape(bp, sp_len, KV_HEADS, HEAD_DIM))

        attn_p = _attention(qp, kp.reshape(bp, sp_len, KV_W), vp.reshape(bp, sp_len, KV_W), aog, sp_len)
        k_all = jnp.concatenate([cache_k[:, l].reshape(bs, -1, KV_W).astype(BF16),
                                 ks.reshape(bs, ss_len, KV_W)], axis=1)
        v_all = jnp.concatenate([cache_v[:, l].reshape(bs, -1, KV_W).astype(BF16),
                                 vs.reshape(bs, ss_len, KV_W)], axis=1)
        attn_s = _attention(qs, k_all, v_all, aog, ss_len)

        x1, h2, idx_l, gates, rank_l, cnt = _outproj(attn_p, attn_s, gmp, gms, w_out_bf, xp, xs, mod4,
                                                     row_all, n2, wr_pad, br_pad)

        idx = idx_l[:, :TOP_K]
        rank = rank_l[:, :TOP_K]
        counts = cnt[0, :N_EXPERTS].astype(jnp.int32)
        padded = ((counts + SB - 1) // SB) * SB
        pad_end = jnp.cumsum(padded)
        pad_start = pad_end - padded
        dest = (pad_start[idx] + rank).astype(jnp.int32)
        dest_flat = dest.reshape(-1)
        items_e = (padded + RB - 1) // RB
        items_end = jnp.cumsum(items_e)
        j = jnp.arange(n_items, dtype=jnp.int32)
        item_e = jnp.minimum(jnp.sum((items_end[None, :] <= j[:, None]).astype(jnp.int32), axis=1),
                             N_EXPERTS - 1)
        within = j - (items_end - items_e)[item_e]
        item_row = pad_start[item_e] + within * RB
        item_rows = jnp.clip(padded[item_e] - within * RB, 0, RB)
        item_nsb = jnp.where(j < items_end[-1], item_rows // SB, 0).astype(jnp.int32)
        item_row = jnp.where(item_nsb > 0, item_row, 0).astype(jnp.int32)

        nblk = (pad_end[-1:] // SB).astype(jnp.int32)
        x_sorted = _scatter_rows(dest_flat, (pad_start + counts).astype(jnp.int32),
                                 pad_end.astype(jnp.int32), nblk, h2, n_pad)
        framed = lambda a: jnp.pad(a.astype(jnp.int32), (1, 1))
        y_sorted = _moe(framed(item_e), framed(item_row), framed(item_nsb), nblk, x_sorted,
                        w_gate[l], w_up[l], w_down[l], b_gate[l], b_up[l], b_down[l], n_items)

        nf = norm_f.reshape(1, D_MODEL)
        is_last = l == depth - 1
        assert is_last, "only DEPTH == 1 is supported"
        dest_flat = dest.reshape(-1)
        y_prompt = _combine(dest_flat, y_sorted, gates, x1, mod4, row_p, nf, 0, n_p)
        y_sample = _combine(dest_flat, y_sorted, gates, x1, mod4, row_s, nf, n_p, n_s)

    return (y_prompt.reshape(x_prompt.shape), y_sample.reshape(x_sample.shape),
            jnp.stack(new_k, axis=1), jnp.stack(new_v, axis=1))
```

```python
import functools

import jax
import jax.numpy as jnp
from jax import lax
from jax.experimental import pallas as pl
from jax.experimental.pallas import tpu as pltpu

F32 = jnp.float32
BF16 = jnp.bfloat16

D_MODEL = 2048
GRID_W = 64
N_HEADS = 8
KV_HEADS = 2
HEAD_DIM = 128
GQA_GROUP = N_HEADS // KV_HEADS
ATTN_W = N_HEADS * HEAD_DIM
KV_W = KV_HEADS * HEAD_DIM
G_HEADS = 8
G_DIM = (D_MODEL - ATTN_W) // G_HEADS
GMLP_W = G_HEADS * G_DIM
IN_W = ATTN_W + 2 * KV_W + 2 * GMLP_W
CHUNK = 128
ROPE_THETA = 10000.0
AXIS_DIM = HEAD_DIM // 2
N_EXPERTS = 32
TOP_K = 4
D_FF = D_MODEL
SWIGLU_ALPHA = 1.702
SWIGLU_LIMIT = 7.0
N_MOD = 6
EPS = 1e-6

LANES = 128
MIB = 1024 * 1024

TM = 256
TQ = 256
KCHUNK = 1152
TC = 128
MOD_TN = 1024
SB = 256
RB = 1792
TF = 256
NF = D_FF // TF
BIG_NSB = (RB // SB, RB // SB - 1)
GLAG = 2
GSLOTS = GLAG + 1
NEG = -1e30


def _rms(x):
    return x * lax.rsqrt(jnp.mean(x * x, axis=-1, keepdims=True) + EPS)


def _aligned(x, m):
    if isinstance(x, int):
        assert x % m == 0
        return x
    return pl.multiple_of(x, m)


def _mod_kernel(c_ref, w_ref, b_ref, o_ref):
    c = c_ref[...]
    a = c * jax.nn.sigmoid(c)
    o_ref[...] = jnp.dot(a, w_ref[...], preferred_element_type=F32,
                         precision=lax.Precision.HIGHEST) + b_ref[...]


def _modulation(cvec8, w_mod, b_mod):
    n = w_mod.shape[1]
    return pl.pallas_call(
        _mod_kernel,
        out_shape=jax.ShapeDtypeStruct((8, n), F32),
        grid=(n // MOD_TN,),
        in_specs=[pl.BlockSpec((8, D_MODEL), lambda j: (0, 0)),
                  pl.BlockSpec((D_MODEL, MOD_TN), lambda j: (0, j)),
                  pl.BlockSpec((1, MOD_TN), lambda j: (0, j))],
        out_specs=pl.BlockSpec((8, MOD_TN), lambda j: (0, j)),
        compiler_params=pltpu.CompilerParams(dimension_semantics=("arbitrary",),
                                             vmem_limit_bytes=40 * MIB),
        name="modulation",
    )(cvec8, w_mod, b_mod.reshape(1, n))


def _mod_spec(part, row_fn):
    return pl.BlockSpec((None, None, 1, D_MODEL), lambda i, *_: (row_fn(i), part, 0, 0))


def _inproj_kernel(*refs, rope, emit_kv):
    it = iter(refs)
    x_ref, shift_ref, scale_ref, norm_ref, w_ref, qg_ref, kg_ref = (next(it) for _ in range(7))
    cos_ref = sin_ref = None
    if rope:
        cos_ref, sin_ref = next(it), next(it)
    wsp_ref, bsp_ref, gog_ref = next(it), next(it), next(it)
    q_out, k_out, v_out, gm_out = (next(it) for _ in range(4))
    kf_out = vf_out = None
    if emit_kv:
        kf_out, vf_out = next(it), next(it)
    gm_scr = next(it)

    x = x_ref[...]
    h = _rms(x) * (norm_ref[...] * (1.0 + scale_ref[...])) + shift_ref[...]
    z = jnp.dot(h.astype(BF16), w_ref[...], preferred_element_type=F32)

    if rope:
        cosv = cos_ref[...]
        sinv = sin_ref[...]
        lane = lax.broadcasted_iota(jnp.int32, (TM, HEAD_DIM), 1)
        first_half = (lane % AXIS_DIM) < (AXIS_DIM // 2)

    def head(zc, gain):
        n = _rms(zc) * gain
        if rope:
            partner = jnp.where(first_half, pltpu.roll(n, HEAD_DIM - AXIS_DIM // 2, 1),
                                pltpu.roll(n, AXIS_DIM // 2, 1))
            n = n * cosv + partner * sinv
        return n

    qg = qg_ref[...]
    kg = kg_ref[...]
    sm_scale = HEAD_DIM ** -0.5
    for hh in range(N_HEADS):
        sl = slice(hh * HEAD_DIM, (hh + 1) * HEAD_DIM)
        q_out[:, sl] = (head(z[:, sl], qg) * sm_scale).astype(BF16)
    for hh in range(KV_HEADS):
        sl = slice(hh * HEAD_DIM, (hh + 1) * HEAD_DIM)
        kh = head(z[:, ATTN_W + hh * HEAD_DIM:ATTN_W + (hh + 1) * HEAD_DIM], kg)
        k_out[:, sl] = kh.astype(BF16)
        if emit_kv:
            kf_out[:, sl] = kh
    v = z[:, ATTN_W + KV_W:ATTN_W + 2 * KV_W]
    v_out[...] = v.astype(BF16)
    if emit_kv:
        vf_out[...] = v

    u0 = ATTN_W + 2 * KV_W
    g0 = u0 + GMLP_W
    for hh in range(G_HEADS):
        sl = slice(hh * G_DIM, (hh + 1) * G_DIM)
        u = jax.nn.gelu(z[:, u0 + hh * G_DIM:u0 + (hh + 1) * G_DIM])
        g = _rms(jax.nn.gelu(z[:, g0 + hh * G_DIM:g0 + (hh + 1) * G_DIM])).astype(BF16)
        w_h = wsp_ref[hh]
        b_h = bsp_ref[hh]
        for c in range(TM // CHUNK):
            rs = slice(c * CHUNK, (c + 1) * CHUNK)
            sp = jnp.dot(w_h, g[rs], preferred_element_type=F32) + b_h
            gm_scr[rs, sl] = u[rs] * sp
    gm_out[...] = (_rms(gm_scr[...]) * gog_ref[...]).astype(BF16)


def _inproj(x2d, mod4, row_fn, norm1, w_in_bf, q_gain, k_gain, rope_tabs, wsp_bf, bsp_b, gog,
            seq_len, emit_kv):
    t = x2d.shape[0]
    rope = rope_tabs is not None
    tiles_per_seq = seq_len // TM
    const = lambda i: (0, 0)
    in_specs = [pl.BlockSpec((TM, D_MODEL), lambda i: (i, 0)),
                _mod_spec(0, row_fn), _mod_spec(1, row_fn),
                pl.BlockSpec((1, D_MODEL), const),
                pl.BlockSpec((D_MODEL, IN_W), const, pipeline_mode=pl.Buffered(1)),
                pl.BlockSpec((1, HEAD_DIM), const),
                pl.BlockSpec((1, HEAD_DIM), const)]
    args = [x2d, mod4, mod4, norm1, w_in_bf, q_gain, k_gain]
    if rope:
        tab_spec = pl.BlockSpec((TM, HEAD_DIM), lambda i: (i % tiles_per_seq, 0))
        in_specs += [tab_spec, tab_spec]
        args += list(rope_tabs)
    in_specs += [pl.BlockSpec((G_HEADS, CHUNK, CHUNK), lambda i: (0, 0, 0)),
                 pl.BlockSpec((G_HEADS, CHUNK, G_DIM), lambda i: (0, 0, 0)),
                 pl.BlockSpec((1, GMLP_W), const)]
    args += [wsp_bf, bsp_b, gog]
    row = lambda w: pl.BlockSpec((TM, w), lambda i: (i, 0))
    out_shape = [jax.ShapeDtypeStruct((t, ATTN_W), BF16), jax.ShapeDtypeStruct((t, KV_W), BF16),
                 jax.ShapeDtypeStruct((t, KV_W), BF16), jax.ShapeDtypeStruct((t, GMLP_W), BF16)]
    out_specs = [row(ATTN_W), row(KV_W), row(KV_W), row(GMLP_W)]
    if emit_kv:
        out_shape += [jax.ShapeDtypeStruct((t, KV_W), F32)] * 2
        out_specs += [row(KV_W), row(KV_W)]
    return pl.pallas_call(
        functools.partial(_inproj_kernel, rope=rope, emit_kv=emit_kv),
        out_shape=out_shape,
        grid=(t // TM,),
        in_specs=in_specs,
        out_specs=out_specs,
        scratch_shapes=[pltpu.VMEM((TM, GMLP_W), F32)],
        compiler_params=pltpu.CompilerParams(dimension_semantics=("arbitrary",),
                                             vmem_limit_bytes=48 * MIB),
        name="inproj_rope" if rope else "inproj",
    )(*args)


def _attn_kernel(q_ref, k_ref, v_ref, gain_ref, o_ref, *, chunks):
    outs = [None] * N_HEADS
    for kvh in range(KV_HEADS):
        ks = slice(kvh * HEAD_DIM, (kvh + 1) * HEAD_DIM)
        q4 = jnp.concatenate(
            [q_ref[:, (kvh * GQA_GROUP + g) * HEAD_DIM:(kvh * GQA_GROUP + g + 1) * HEAD_DIM]
             for g in range(GQA_GROUP)], axis=0)
        m = l = acc = None
        for c0, cl in chunks:
            s = lax.dot_general(q4, k_ref[c0:c0 + cl, ks], (((1,), (1,)), ((), ())),
                                preferred_element_type=F32)
            mc = jnp.max(s, axis=-1, keepdims=True)
            m_new = mc if m is None else jnp.maximum(m, mc)
            p = jnp.exp(s - m_new)
            ps = jnp.sum(p, axis=-1, keepdims=True)
            pv = jnp.dot(p.astype(BF16), v_ref[c0:c0 + cl, ks], preferred_element_type=F32)
            if m is None:
                l, acc = ps, pv
            else:
                corr = jnp.exp(m - m_new)
                l = l * corr + ps
                acc = acc * corr + pv
            m = m_new
        o = acc / l
        for g in range(GQA_GROUP):
            outs[kvh * GQA_GROUP + g] = o[g * TQ:(g + 1) * TQ]
    ss = outs[0] * outs[0]
    for hh in range(1, N_HEADS):
        ss = ss + outs[hh] * outs[hh]
    inv = lax.rsqrt(jnp.sum(ss, axis=-1, keepdims=True) * (1.0 / ATTN_W) + EPS)
    gain = gain_ref[...]
    for hh in range(N_HEADS):
        sl = slice(hh * HEAD_DIM, (hh + 1) * HEAD_DIM)
        o_ref[:, sl] = (outs[hh] * inv * gain[:, sl]).astype(BF16)


def _attention(q2d, k3d, v3d, gain, seq_len):
    t = q2d.shape[0]
    b, l, _ = k3d.shape
    nq = seq_len // TQ
    n_chunks = -(-l // KCHUNK)
    base = (l // LANES) // n_chunks
    extra = (l // LANES) % n_chunks
    sizes = [(base + (1 if c < extra else 0)) * LANES for c in range(n_chunks)]
    chunks = tuple((sum(sizes[:c]), sizes[c]) for c in range(n_chunks))
    return pl.pallas_call(
        functools.partial(_attn_kernel, chunks=chunks),
        out_shape=jax.ShapeDtypeStruct((t, ATTN_W), BF16),
        grid=(b, nq),
        in_specs=[pl.BlockSpec((TQ, ATTN_W), lambda bi, i: (bi * nq + i, 0)),
                  pl.BlockSpec((None, l, KV_W), lambda bi, i: (bi, 0, 0)),
                  pl.BlockSpec((None, l, KV_W), lambda bi, i: (bi, 0, 0)),
                  pl.BlockSpec((1, ATTN_W), lambda bi, i: (0, 0))],
        out_specs=pl.BlockSpec((TQ, ATTN_W), lambda bi, i: (bi * nq + i, 0)),
        compiler_params=pltpu.CompilerParams(dimension_semantics=("arbitrary", "arbitrary"),
                                             vmem_limit_bytes=48 * MIB),
        name="attention",
    )(q2d, k3d, v3d, gain)


def _outproj_kernel(ap_ref, as_ref, gmp_ref, gms_ref, w_ref, xp_ref, xs_ref, g1_ref, n2_ref, sc2_ref,
                    sh2_ref, wr_ref, br_ref, x1_out, h2_out, idx_out, gate_out, rank_out, cnt_out,
                    cnt_scr, lg_scr, *, prompt_tiles, n_tiles):
    i = pl.program_id(0)
    tile = jnp.minimum(i, n_tiles - 1)
    is_p = tile < prompt_tiles

    @pl.when(i == 0)
    def _():
        cnt_scr[...] = jnp.zeros_like(cnt_scr)
        lg_scr[...] = jnp.full(lg_scr.shape, NEG, F32)

    logits = lg_scr[(i + 1) % 2]
    lane = lax.broadcasted_iota(jnp.int32, (TM, LANES), 1)
    lane_f = lane.astype(F32)
    work = logits
    vals, sels = [], []
    multi = jnp.zeros((TM, LANES), F32)
    idx_acc = jnp.zeros((TM, LANES), F32)
    for k in range(TOP_K):
        m = jnp.max(work, axis=-1, keepdims=True)
        idx = jnp.min(jnp.where(work == m, lane_f, float(LANES)), axis=-1, keepdims=True)
        sel = lane_f == idx
        vals.append(m)
        sels.append(sel)
        work = jnp.where(sel, -jnp.inf, work)
        multi = multi + sel.astype(F32)
        idx_acc = jnp.where(lane == k, idx, idx_acc)
    es = [jnp.exp(v - vals[0]) for v in vals]
    den = es[0] + es[1] + es[2] + es[3]
    gate_acc = jnp.zeros((TM, LANES), F32)
    for k in range(TOP_K):
        gate_acc = jnp.where(lane == k, es[k] / den, gate_acc)

    r_i = lax.broadcasted_iota(jnp.int32, (TM, TM), 0)
    c_i = lax.broadcasted_iota(jnp.int32, (TM, TM), 1)
    lower = (c_i < r_i).astype(BF16)
    prefix = jnp.dot(lower, multi.astype(BF16), preferred_element_type=F32) + cnt_scr[...]
    rank_acc = jnp.zeros((TM, LANES), F32)
    for k in range(TOP_K):
        r = jnp.sum(jnp.where(sels[k], prefix, 0.0), axis=-1, keepdims=True)
        rank_acc = jnp.where(lane == k, r, rank_acc)
    cnt_new = cnt_scr[...] + jnp.where(i >= 1, jnp.sum(multi, axis=0, keepdims=True), 0.0)
    cnt_scr[...] = cnt_new
    cnt_out[...] = cnt_new
    idx_out[...] = idx_acc.astype(jnp.int32)
    gate_out[...] = gate_acc
    rank_out[...] = rank_acc.astype(jnp.int32)

    a = jnp.where(is_p, ap_ref[...], as_ref[...])
    gm = jnp.where(is_p, gmp_ref[...], gms_ref[...])
    x = jnp.where(is_p, xp_ref[...], xs_ref[...])
    o = (jnp.dot(a, w_ref[0:ATTN_W, :], preferred_element_type=F32)
         + jnp.dot(gm, w_ref[ATTN_W:, :], preferred_element_type=F32))
    x1 = x + g1_ref[...] * o
    x1_out[...] = x1
    h2 = _rms(x1) * (n2_ref[...] * (1.0 + sc2_ref[...])) + sh2_ref[...]
    h2_out[...] = h2

    h_hi = h2.astype(BF16)
    h_lo = (h2 - h_hi.astype(F32)).astype(BF16)
    wr = wr_ref[...]
    w_hi = wr.astype(BF16)
    w_lo = (wr - w_hi.astype(F32)).astype(BF16)
    r2 = jnp.dot(h_hi, jnp.concatenate([w_hi, w_lo], axis=1), preferred_element_type=F32)
    lg_scr[i % 2] = (r2[:, 0:LANES] + r2[:, LANES:2 * LANES]
                     + jnp.dot(h_lo, w_hi, preferred_element_type=F32)) + br_ref[...]


def _outproj(attn_p, attn_s, gm_p, gm_s, w_out_bf, xp, xs, mod4, row_fn, norm2, wr_pad, br_pad):
    ntp = xp.shape[0] // TM
    t = xp.shape[0] + xs.shape[0]
    n_tiles = t // TM
    const = lambda i: (0, 0)
    cur = lambda i: jnp.minimum(i, n_tiles - 1)
    row = lambda w: pl.BlockSpec((TM, w), lambda i: (cur(i), 0))
    routed = lambda w: pl.BlockSpec((TM, w), lambda i: (jnp.maximum(i - 1, 0), 0))
    row_p = lambda w: pl.BlockSpec((TM, w), lambda i: (jnp.minimum(i, ntp - 1), 0))
    row_s = lambda w: pl.BlockSpec((TM, w), lambda i: (jnp.maximum(cur(i) - ntp, 0), 0))
    mod = lambda part: _mod_spec(part, lambda i: row_fn(cur(i)))
    return pl.pallas_call(
        functools.partial(_outproj_kernel, prompt_tiles=ntp, n_tiles=n_tiles),
        out_shape=[jax.ShapeDtypeStruct((t, D_MODEL), F32),
                   jax.ShapeDtypeStruct((t, D_MODEL), F32),
                   jax.ShapeDtypeStruct((t, LANES), jnp.int32),
                   jax.ShapeDtypeStruct((t, LANES), F32),
                   jax.ShapeDtypeStruct((t, LANES), jnp.int32),
                   jax.ShapeDtypeStruct((1, LANES), F32)],
        grid=(n_tiles + 1,),
        in_specs=[row_p(ATTN_W), row_s(ATTN_W), row_p(GMLP_W), row_s(GMLP_W),
                  pl.BlockSpec((D_MODEL, D_MODEL), const, pipeline_mode=pl.Buffered(1)),
                  row_p(D_MODEL), row_s(D_MODEL),
                  mod(2),
                  pl.BlockSpec((1, D_MODEL), const),
                  mod(4), mod(3),
                  pl.BlockSpec((D_MODEL, LANES), const),
                  pl.BlockSpec((1, LANES), const)],
        out_specs=[row(D_MODEL),
                   row(D_MODEL),
                   routed(LANES), routed(LANES), routed(LANES),
                   pl.BlockSpec((1, LANES), const)],
        scratch_shapes=[pltpu.VMEM((1, LANES), F32), pltpu.VMEM((2, TM, LANES), F32)],
        compiler_params=pltpu.CompilerParams(dimension_semantics=("arbitrary",),
                                             vmem_limit_bytes=48 * MIB),
        name="outproj_router",
    )(attn_p, attn_s, gm_p, gm_s, w_out_bf, xp, xs, mod4, norm2, mod4, mod4, wr_pad, br_pad)


def _scatter_rows_kernel(dest_ref, cend_ref, pend_ref, nblk_ref, h_hbm, x_hbm, ring, zbuf,
                         sem_in, sem_sc, sem_z, *, n_sub):
    i = pl.program_id(0)
    n = pl.num_programs(0)

    def in_copy(b):
        return pltpu.make_async_copy(h_hbm.at[pl.ds(_aligned(b * TC, TC), TC), :], ring.at[b % GSLOTS],
                                     sem_in.at[b % GSLOTS])

    def sc_wait(b):
        for _ in range(TOP_K):
            pltpu.make_async_copy(ring.at[b % GSLOTS], x_hbm.at[pl.ds(0, TC), :],
                                  sem_sc.at[b % GSLOTS]).wait()

    def zero_row(r):
        return pltpu.make_async_copy(zbuf.at[pl.ds(0, 1), :], x_hbm.at[pl.ds(r, 1), :], sem_z)

    def zero_block(b):
        return pltpu.make_async_copy(zbuf, x_hbm.at[pl.ds(_aligned(b * SB, SB), SB), :], sem_z)

    @pl.when(i == 0)
    def _():
        for b in range(GLAG):
            in_copy(b).start()
        zbuf[...] = jnp.zeros_like(zbuf)

        def per_expert(ex, carry):
            def start_row(r, c2):
                zero_row(r).start()
                return c2
            lax.fori_loop(cend_ref[ex], pend_ref[ex], start_row, 0)
            return carry
        lax.fori_loop(0, N_EXPERTS, per_expert, 0)

        def start_blk(b, carry):
            zero_block(b).start()
            return carry
        lax.fori_loop(nblk_ref[0], n_sub, start_blk, 0)

    in_copy(i).wait()
    slot = i % GSLOTS
    for j in range(TC):
        for k in range(TOP_K):
            r = dest_ref[(i * TC + j) * TOP_K + k]
            pltpu.make_async_copy(ring.at[slot, pl.ds(j, 1), :], x_hbm.at[pl.ds(r, 1), :],
                                  sem_sc.at[slot]).start()

    @pl.when(i >= 1)
    def _():
        sc_wait(i - 1)

    @pl.when(i + GLAG < n)
    def _():
        in_copy(i + GLAG).start()

    @pl.when(i == n - 1)
    def _():
        sc_wait(i)

        def per_expert(ex, carry):
            def wait_row(r, c2):
                zero_row(0).wait()
                return c2
            lax.fori_loop(cend_ref[ex], pend_ref[ex], wait_row, 0)
            return carry
        lax.fori_loop(0, N_EXPERTS, per_expert, 0)

        def wait_blk(b, carry):
            zero_block(0).wait()
            return carry
        lax.fori_loop(nblk_ref[0], n_sub, wait_blk, 0)


def _scatter_rows(dest_flat, cnt_end, pad_end, nblk, h_words, n_pad):
    t = h_words.shape[0]
    any_spec = pl.BlockSpec(memory_space=pl.ANY)
    return pl.pallas_call(
        functools.partial(_scatter_rows_kernel, n_sub=n_pad // SB),
        out_shape=jax.ShapeDtypeStruct((n_pad, D_MODEL), F32),
        grid_spec=pltpu.PrefetchScalarGridSpec(
            num_scalar_prefetch=4,
            grid=(t // TC,),
            in_specs=[any_spec],
            out_specs=any_spec,
            scratch_shapes=[pltpu.VMEM((GSLOTS, TC, D_MODEL), F32),
                            pltpu.VMEM((SB, D_MODEL), F32),
                            pltpu.SemaphoreType.DMA((GSLOTS,)),
                            pltpu.SemaphoreType.DMA((GSLOTS,)),
                            pltpu.SemaphoreType.DMA]),
        compiler_params=pltpu.CompilerParams(dimension_semantics=("arbitrary",),
                                             vmem_limit_bytes=32 * MIB),
        name="scatter_rows",
    )(dest_flat, cnt_end, pad_end, nblk, h_words)


def _moe_kernel(ie_ref, irow_ref, insb_ref, nblk_ref,
                x_hbm, wg_hbm, wu_hbm, wd_hbm, bgu_ref, bd_ref,
                y_hbm,
                xring, x2d, act_buf, wdb, wgus, wds, wgub, ystage, sem_x, sem_w, sem_y, *, n_sub):
    g = pl.program_id(0)
    nsb = insb_ref[g]
    e = ie_ref[g]
    row0 = irow_ref[g]
    nsb_next = insb_ref[g + 1]
    row0_next = irow_ref[g + 1]
    e_next = ie_ref[g + 1]

    nsb_prev = insb_ref[jnp.maximum(g - 1, 0)]

    def x_copy(src_row, c):
        return pltpu.make_async_copy(x_hbm.at[pl.ds(_aligned(src_row + c * SB, SB), SB), :],
                                     xring.at[c % 2], sem_x.at[c % 2])

    def land(src_row, c, n_total):
        x_copy(src_row, c).wait()
        x2d[pl.ds(_aligned(c * SB, SB), SB), :] = xring[c % 2].astype(BF16)

        @pl.when(c + 2 < n_total)
        def _():
            x_copy(src_row, c + 2).start()

    def issue_first(src_row, n_total):
        for c in range(2):
            pl.when(c < n_total)(lambda c=c: x_copy(src_row, c).start())

    def w_copies(ex, f, slot):
        cols = pl.ds(_aligned(f * TF, TF), TF)
        return (pltpu.make_async_copy(wg_hbm.at[ex, :, cols], wgus.at[slot, 0], sem_w.at[slot, 0]),
                pltpu.make_async_copy(wu_hbm.at[ex, :, cols], wgus.at[slot, 1], sem_w.at[slot, 1]),
                pltpu.make_async_copy(wd_hbm.at[ex, cols, :], wds.at[slot], sem_w.at[slot, 2]))

    def y_copy(sb, slot):
        return pltpu.make_async_copy(ystage.at[slot],
                                     y_hbm.at[pl.ds(_aligned(row0 + sb * SB, SB), SB), :], sem_y.at[slot])

    def catch_up(c, carry):
        land(row0, c, nsb)
        return carry
    lax.fori_loop(jnp.minimum(nsb_prev, nsb), nsb, catch_up, 0)

    @pl.when(nsb == 0)
    def _():
        issue_first(row0_next, nsb_next)

    @pl.when(nsb > 0)
    def _():

        @pl.when(g == 1)
        def _():
            for cp in w_copies(e, 0, 0):
                cp.start()

        def gate_up_tile(f, carry):
            slot = f % 2

            @pl.when(f + 1 < NF)
            def _():
                for cp in w_copies(e, f + 1, 1 - slot):
                    cp.start()
            for cp in w_copies(e, f, slot):
                cp.wait()
            wgub[:, 0:TF] = wgus[slot, 0].astype(BF16)
            wgub[:, TF:2 * TF] = wgus[slot, 1].astype(BF16)
            wdb[pl.ds(_aligned(f * TF, TF), TF), :] = wds[slot].astype(BF16)
            b_gu = bgu_ref[pl.ds(e * NF + f, 1), :]

            def gate_up(row_start, m):
                rows = pl.ds(row_start, m)
                gu = jnp.dot(x2d[rows, :], wgub[...], preferred_element_type=F32) + b_gu
                gt = jnp.minimum(gu[:, 0:TF], SWIGLU_LIMIT)
                up = jnp.clip(gu[:, TF:2 * TF], -SWIGLU_LIMIT, SWIGLU_LIMIT)
                act = (up + 1.0) * (gt * jax.nn.sigmoid(SWIGLU_ALPHA * gt))
                act_buf[f, rows, :] = act.astype(BF16)

            for big in BIG_NSB:
                @pl.when(nsb == big)
                def _(big=big):
                    half = big * SB // 2
                    gate_up(0, half)
                    gate_up(half, half)

            @pl.when(functools.reduce(lambda a, b: a & b, [nsb != big for big in BIG_NSB]))
            def _():
                def one(sb, c2):
                    gate_up(_aligned(sb * SB, SB), SB)
                    return c2
                lax.fori_loop(0, nsb, one, 0)
            return carry
        lax.fori_loop(0, NF, gate_up_tile, 0)

        @pl.when(nsb_next > 0)
        def _():
            for cp in w_copies(e_next, 0, 0):
                cp.start()

        issue_first(row0_next, nsb_next)

        def down(sb, carry):
            yslot = sb % 2

            @pl.when(sb >= 2)
            def _():
                y_copy(sb - 2, yslot).wait()

            @pl.when(sb < nsb_next)
            def _():
                land(row0_next, sb, nsb_next)
            rows = pl.ds(_aligned(sb * SB, SB), SB)
            a = jnp.concatenate([act_buf[f, rows, :] for f in range(NF)], axis=1)
            ystage[yslot] = jnp.dot(a, wdb[...], preferred_element_type=F32) + bd_ref[pl.ds(e, 1), :]
            y_copy(sb, yslot).start()
            return carry
        lax.fori_loop(0, nsb, down, 0)
        y_copy(nsb - 1, (nsb - 1) % 2).wait()

        @pl.when(nsb >= 2)
        def _():
            y_copy(nsb - 2, nsb % 2).wait()

    @pl.when(g == pl.num_programs(0) - 1)
    def _():
        ystage[0] = jnp.zeros((SB, D_MODEL), F32)

        def fill(b):
            return pltpu.make_async_copy(ystage.at[0], y_hbm.at[pl.ds(_aligned(b * SB, SB), SB), :],
                                         sem_y.at[0])

        def start_fill(b, carry):
            fill(b).start()
            return carry

        def wait_fill(b, carry):
            fill(b).wait()
            return carry
        lax.fori_loop(nblk_ref[0], n_sub, start_fill, 0)
        lax.fori_loop(nblk_ref[0], n_sub, wait_fill, 0)


def _moe(item_e, item_row, item_nsb, nblk, x_sorted, w_gate, w_up, w_down, b_gate, b_up, b_down,
         n_items):
    any_spec = pl.BlockSpec(memory_space=pl.ANY)
    b_gu = jnp.concatenate([b_gate.reshape(N_EXPERTS * NF, TF), b_up.reshape(N_EXPERTS * NF, TF)], axis=1)
    grid_spec = pltpu.PrefetchScalarGridSpec(
        num_scalar_prefetch=4,
        grid=(n_items + 1,),
        in_specs=[any_spec, any_spec, any_spec, any_spec,
                  pl.BlockSpec((N_EXPERTS * NF, 2 * TF), lambda i, *_: (0, 0)),
                  pl.BlockSpec((N_EXPERTS, D_MODEL), lambda i, *_: (0, 0))],
        out_specs=any_spec,
        scratch_shapes=[pltpu.VMEM((2, SB, D_MODEL), F32),
                        pltpu.VMEM((RB, D_MODEL), BF16),
                        pltpu.VMEM((NF, RB, TF), BF16),
                        pltpu.VMEM((D_FF, D_MODEL), BF16),
                        pltpu.VMEM((2, 2, D_MODEL, TF), F32),
                        pltpu.VMEM((2, TF, D_MODEL), F32),
                        pltpu.VMEM((D_MODEL, 2 * TF), BF16),
                        pltpu.VMEM((2, SB, D_MODEL), F32),
                        pltpu.SemaphoreType.DMA((2,)),
                        pltpu.SemaphoreType.DMA((2, 3)),
                        pltpu.SemaphoreType.DMA((2,))],
    )
    return pl.pallas_call(
        functools.partial(_moe_kernel, n_sub=x_sorted.shape[0] // SB),
        out_shape=jax.ShapeDtypeStruct((x_sorted.shape[0], D_MODEL), F32),
        grid_spec=grid_spec,
        compiler_params=pltpu.CompilerParams(dimension_semantics=("arbitrary",),
                                             vmem_limit_bytes=56 * MIB),
        name="moe_experts",
    )(item_e, item_row, item_nsb, nblk, x_sorted, w_gate, w_up, w_down, b_gu, b_down)


def _combine_kernel(dest_ref, y_hbm, gate_ref, x1_ref, g2_ref, nf_ref, o_ref, ybuf, sem, *, tok_off):
    i = pl.program_id(0)
    n = pl.num_programs(0)

    def issue(b):
        slot = b % GSLOTS
        for j in range(TC):
            for k in range(TOP_K):
                r = dest_ref[(tok_off + b * TC + j) * TOP_K + k]
                pltpu.make_async_copy(y_hbm.at[pl.ds(r, 1), :], ybuf.at[slot, k, pl.ds(j, 1), :],
                                      sem.at[slot]).start()

    @pl.when(i == 0)
    def _():
        for b in range(GLAG):
            issue(b)

    pl.when(i + GLAG < n)(lambda: issue(i + GLAG))

    slot = i % GSLOTS
    for k in range(TOP_K):
        pltpu.make_async_copy(y_hbm.at[pl.ds(0, TC), :], ybuf.at[slot, k], sem.at[slot]).wait()
    gate = gate_ref[...]
    f = gate[:, 0:1] * ybuf[slot, 0]
    for k in range(1, TOP_K):
        f = f + gate[:, k:k + 1] * ybuf[slot, k]
    x2 = x1_ref[...] + g2_ref[...] * f
    o_ref[...] = _rms(x2) * nf_ref[...]


def _combine(dest, y_sorted, gates, x1, mod4, row_fn, norm_f, tok_off, t):
    off = tok_off // TC
    row_in = lambda w: pl.BlockSpec((TC, w), lambda i, *_: (off + i, 0))
    return pl.pallas_call(
        functools.partial(_combine_kernel, tok_off=tok_off),
        out_shape=jax.ShapeDtypeStruct((t, D_MODEL), F32),
        grid_spec=pltpu.PrefetchScalarGridSpec(
            num_scalar_prefetch=1,
            grid=(t // TC,),
            in_specs=[pl.BlockSpec(memory_space=pl.ANY), row_in(LANES), row_in(D_MODEL),
                      pl.BlockSpec((None, None, 1, D_MODEL),
                                   lambda i, *_: (row_fn(i * TC // TM), 5, 0, 0)),
                      pl.BlockSpec((1, D_MODEL), lambda i, *_: (0, 0))],
            out_specs=pl.BlockSpec((TC, D_MODEL), lambda i, *_: (i, 0)),
            scratch_shapes=[pltpu.VMEM((GSLOTS, TOP_K, TC, D_MODEL), F32),
                            pltpu.SemaphoreType.DMA((GSLOTS,))]),
        compiler_params=pltpu.CompilerParams(dimension_semantics=("arbitrary",),
                                             vmem_limit_bytes=40 * MIB),
        name="combine_norm",
    )(dest, y_sorted, gates, x1, mod4, norm_f)


def _rope_tables(n_tokens):
    n_rows = n_tokens // GRID_W
    row = jnp.repeat(jnp.arange(n_rows, dtype=F32), GRID_W)
    col = jnp.tile(jnp.arange(GRID_W, dtype=F32), n_rows)
    half = AXIS_DIM // 2
    inv_freq = ROPE_THETA ** (-jnp.arange(half, dtype=F32) / half)
    ang_r = row[:, None] * inv_freq[None, :]
    ang_c = col[:, None] * inv_freq[None, :]
    cos_t = jnp.concatenate([jnp.cos(ang_r)] * 2 + [jnp.cos(ang_c)] * 2, axis=-1)
    sin_t = jnp.concatenate([-jnp.sin(ang_r), jnp.sin(ang_r), -jnp.sin(ang_c), jnp.sin(ang_c)], axis=-1)
    return cos_t, sin_t


def kernel(x_prompt, x_sample, cache_k, cache_v, c, c_ctx, w_mod, b_mod, norm1, w_in, q_gain, k_gain,
           w_sp, b_sp, attn_out_gain, gmlp_out_gain, w_out, norm2, w_router, b_router,
           w_gate, b_gate, w_up, b_up, w_down, b_down, norm_f):
    bp, sp_len, _ = x_prompt.shape
    bs, ss_len, _ = x_sample.shape
    depth = w_mod.shape[0]
    n_p = bp * sp_len
    n_s = bs * ss_len
    t_total = n_p + n_s
    n_assign = t_total * TOP_K
    n_sub = (n_assign + N_EXPERTS * (SB - 1)) // SB + 1
    n_pad = n_sub * SB
    n_items = N_EXPERTS + -(-n_pad // RB)

    xp = x_prompt.reshape(n_p, D_MODEL)
    xs = x_sample.reshape(n_s, D_MODEL)
    tiles_per_sample = ss_len // TM
    row_p = lambda i: 0
    row_s = lambda i: 1 + i // tiles_per_sample
    row_all = lambda i: jnp.where(i < n_p // TM, 0, 1 + (i - n_p // TM) // tiles_per_sample)
    rope_tabs = _rope_tables(ss_len)
    cvec8 = jnp.zeros((8, D_MODEL), F32).at[0].set(c_ctx).at[1:1 + bs].set(c)
    new_k, new_v = [], []

    for l in range(depth):
        mod4 = _modulation(cvec8, w_mod[l], b_mod[l]).reshape(8, N_MOD, 1, D_MODEL)
        w_in_bf = w_in[l].astype(BF16)
        w_out_bf = w_out[l].astype(BF16)
        wsp_bf = w_sp[l].astype(BF16)
        bsp_b = jnp.broadcast_to(b_sp[l][:, :, None], (G_HEADS, CHUNK, G_DIM))
        n1 = norm1[l].reshape(1, D_MODEL)
        n2 = norm2[l].reshape(1, D_MODEL)
        qg = q_gain[l].reshape(1, HEAD_DIM)
        kg = k_gain[l].reshape(1, HEAD_DIM)
        gog = gmlp_out_gain[l].reshape(1, GMLP_W)
        aog = attn_out_gain[l].reshape(1, ATTN_W)
        wr_pad = jnp.zeros((D_MODEL, LANES), F32).at[:, :N_EXPERTS].set(w_router[l])
        br_pad = jnp.full((1, LANES), NEG, F32).at[0, :N_EXPERTS].set(b_router[l])

        qp, kp, vp, gmp, kpf, vpf = _inproj(xp, mod4, row_p, n1, w_in_bf, qg, kg, None, wsp_bf, bsp_b,
                                            gog, sp_len, True)
        qs, ks, vs, gms = _inproj(xs, mod4, row_s, n1, w_in_bf, qg, kg, rope_tabs, wsp_bf, bsp_b,
                                  gog, ss_len, False)
        new_k.append(kpf.reshape(bp, sp_len, KV_HEADS, HEAD_DIM))
        new_v.append(vpf.reshape(bp, sp_len, KV_HEADS, HEAD_DIM))

        attn_p = _attention(qp, kp.reshape(bp, sp_len, KV_W), vp.reshape(bp, sp_len, KV_W), aog, sp_len)
        k_all = jnp.concatenate([cache_k[:, l].reshape(bs, -1, KV_W).astype(BF16),
                                 ks.reshape(bs, ss_len, KV_W)], axis=1)
        v_all = jnp.concatenate([cache_v[:, l].reshape(bs, -1, KV_W).astype(BF16),
                                 vs.reshape(bs, ss_len, KV_W)], axis=1)
        attn_s = _attention(qs, k_all, v_all, aog, ss_len)

        x1, h2, idx_l, gates, rank_l, cnt = _outproj(attn_p, attn_s, gmp, gms, w_out_bf, xp, xs, mod4,
                                                     row_all, n2, wr_pad, br_pad)

        idx = idx_l[:, :TOP_K]
        rank = rank_l[:, :TOP_K]
        counts = cnt[0, :N_EXPERTS].astype(jnp.int32)
        padded = ((counts + SB - 1) // SB) * SB
        pad_end = jnp.cumsum(padded)
        pad_start = pad_end - padded
        dest = (pad_start[idx] + rank).astype(jnp.int32)
        dest_flat = dest.reshape(-1)
        items_e = (padded + RB - 1) // RB
        items_end = jnp.cumsum(items_e)
        j = jnp.arange(n_items, dtype=jnp.int32)
        item_e = jnp.minimum(jnp.sum((items_end[None, :] <= j[:, None]).astype(jnp.int32), axis=1),
                             N_EXPERTS - 1)
        within = j - (items_end - items_e)[item_e]
        item_row = pad_start[item_e] + within * RB
        item_rows = jnp.clip(padded[item_e] - within * RB, 0, RB)
        item_nsb = jnp.where(j < items_end[-1], item_rows // SB, 0).astype(jnp.int32)
        item_row = jnp.where(item_nsb > 0, item_row, 0).astype(jnp.int32)

        nblk = (pad_end[-1:] // SB).astype(jnp.int32)
        x_sorted = _scatter_rows(dest_flat, (pad_start + counts).astype(jnp.int32),
                                 pad_end.astype(jnp.int32), nblk, h2, n_pad)
        framed = lambda a: jnp.pad(a.astype(jnp.int32), (1, 1))
        y_sorted = _moe(framed(item_e), framed(item_row), framed(item_nsb), nblk, x_sorted,
                        w_gate[l], w_up[l], w_down[l], b_gate[l], b_up[l], b_down[l], n_items)

        nf = norm_f.reshape(1, D_MODEL)
        is_last = l == depth - 1
        assert is_last, "only DEPTH == 1 is supported"
        dest_flat = dest.reshape(-1)
        y_prompt = _combine(dest_flat, y_sorted, gates, x1, mod4, row_p, nf, 0, n_p)
        y_sample = _combine(dest_flat, y_sorted, gates, x1, mod4, row_s, nf, n_p, n_s)

    return (y_prompt.reshape(x_prompt.shape), y_sample.reshape(x_sample.shape),
            jnp.stack(new_k, axis=1), jnp.stack(new_v, axis=1))
```

```python
import functools

import jax
import jax.numpy as jnp
from jax import lax
from jax.experimental import pallas as pl
from jax.experimental.pallas import tpu as pltpu

F32 = jnp.float32
BF16 = jnp.bfloat16

D_MODEL = 2048
GRID_W = 64
N_HEADS = 8
KV_HEADS = 2
HEAD_DIM = 128
GQA_GROUP = N_HEADS // KV_HEADS
ATTN_W = N_HEADS * HEAD_DIM
KV_W = KV_HEADS * HEAD_DIM
G_HEADS = 8
G_DIM = (D_MODEL - ATTN_W) // G_HEADS
GMLP_W = G_HEADS * G_DIM
IN_W = ATTN_W + 2 * KV_W + 2 * GMLP_W
CHUNK = 128
ROPE_THETA = 10000.0
AXIS_DIM = HEAD_DIM // 2
N_EXPERTS = 32
TOP_K = 4
D_FF = D_MODEL
SWIGLU_ALPHA = 1.702
SWIGLU_LIMIT = 7.0
N_MOD = 6
EPS = 1e-6

LANES = 128
MIB = 1024 * 1024

TM = 256
TQ = 256
KCHUNK = 1152
TC = 128
MOD_TN = 1024
SB = 256
RB = 1792
TF = 256
NF = D_FF // TF
BIG_NSB = (RB // SB, RB // SB - 1)
GLAG = 2
GSLOTS = GLAG + 1
NEG = -1e30


def _rms(x):
    return x * lax.rsqrt(jnp.mean(x * x, axis=-1, keepdims=True) + EPS)


def _aligned(x, m):
    if isinstance(x, int):
        assert x % m == 0
        return x
    return pl.multiple_of(x, m)


def _mod_kernel(c_ref, w_ref, b_ref, o_ref):
    c = c_ref[...]
    a = c * jax.nn.sigmoid(c)
    o_ref[...] = jnp.dot(a.astype(BF16), w_ref[...].astype(BF16),
                         preferred_element_type=F32) + b_ref[...]


def _modulation(cvec8, w_mod, b_mod):
    n = w_mod.shape[1]
    return pl.pallas_call(
        _mod_kernel,
        out_shape=jax.ShapeDtypeStruct((8, n), F32),
        grid=(n // MOD_TN,),
        in_specs=[pl.BlockSpec((8, D_MODEL), lambda j: (0, 0)),
                  pl.BlockSpec((D_MODEL, MOD_TN), lambda j: (0, j)),
                  pl.BlockSpec((1, MOD_TN), lambda j: (0, j))],
        out_specs=pl.BlockSpec((8, MOD_TN), lambda j: (0, j)),
        compiler_params=pltpu.CompilerParams(dimension_semantics=("arbitrary",),
                                             vmem_limit_bytes=40 * MIB),
        name="modulation",
    )(cvec8, w_mod, b_mod.reshape(1, n))


def _mod_spec(part, row_fn):
    return pl.BlockSpec((None, None, 1, D_MODEL), lambda i, *_: (row_fn(i), part, 0, 0))


def _inproj_kernel(*refs, rope, emit_kv):
    it = iter(refs)
    x_ref, shift_ref, scale_ref, norm_ref, w_ref, qg_ref, kg_ref = (next(it) for _ in range(7))
    cos_ref = sin_ref = None
    if rope:
        cos_ref, sin_ref = next(it), next(it)
    wsp_ref, bsp_ref, gog_ref = next(it), next(it), next(it)
    q_out, k_out, v_out, gm_out = (next(it) for _ in range(4))
    kf_out = vf_out = None
    if emit_kv:
        kf_out, vf_out = next(it), next(it)
    gm_scr = next(it)

    x = x_ref[...]
    h = _rms(x) * (norm_ref[...] * (1.0 + scale_ref[...])) + shift_ref[...]
    z = jnp.dot(h.astype(BF16), w_ref[...], preferred_element_type=F32)

    if rope:
        cosv = cos_ref[...]
        sinv = sin_ref[...]
        lane = lax.broadcasted_iota(jnp.int32, (TM, HEAD_DIM), 1)
        first_half = (lane % AXIS_DIM) < (AXIS_DIM // 2)

    def head(zc, gain):
        n = _rms(zc) * gain
        if rope:
            partner = jnp.where(first_half, pltpu.roll(n, HEAD_DIM - AXIS_DIM // 2, 1),
                                pltpu.roll(n, AXIS_DIM // 2, 1))
            n = n * cosv + partner * sinv
        return n

    qg = qg_ref[...]
    kg = kg_ref[...]
    sm_scale = HEAD_DIM ** -0.5
    for hh in range(N_HEADS):
        sl = slice(hh * HEAD_DIM, (hh + 1) * HEAD_DIM)
        q_out[:, sl] = (head(z[:, sl], qg) * sm_scale).astype(BF16)
    for hh in range(KV_HEADS):
        sl = slice(hh * HEAD_DIM, (hh + 1) * HEAD_DIM)
        kh = head(z[:, ATTN_W + hh * HEAD_DIM:ATTN_W + (hh + 1) * HEAD_DIM], kg)
        k_out[:, sl] = kh.astype(BF16)
        if emit_kv:
            kf_out[:, sl] = kh
    v = z[:, ATTN_W + KV_W:ATTN_W + 2 * KV_W]
    v_out[...] = v.astype(BF16)
    if emit_kv:
        vf_out[...] = v

    u0 = ATTN_W + 2 * KV_W
    g0 = u0 + GMLP_W
    for hh in range(G_HEADS):
        sl = slice(hh * G_DIM, (hh + 1) * G_DIM)
        u = jax.nn.gelu(z[:, u0 + hh * G_DIM:u0 + (hh + 1) * G_DIM])
        g = _rms(jax.nn.gelu(z[:, g0 + hh * G_DIM:g0 + (hh + 1) * G_DIM])).astype(BF16)
        w_h = wsp_ref[hh]
        b_h = bsp_ref[hh]
        for c in range(TM // CHUNK):
            rs = slice(c * CHUNK, (c + 1) * CHUNK)
            sp = jnp.dot(w_h, g[rs], preferred_element_type=F32) + b_h
            gm_scr[rs, sl] = u[rs] * sp
    gm_out[...] = (_rms(gm_scr[...]) * gog_ref[...]).astype(BF16)


def _inproj(x2d, mod4, row_fn, norm1, w_in_bf, q_gain, k_gain, rope_tabs, wsp_bf, bsp_b, gog,
            seq_len, emit_kv):
    t = x2d.shape[0]
    rope = rope_tabs is not None
    tiles_per_seq = seq_len // TM
    const = lambda i: (0, 0)
    in_specs = [pl.BlockSpec((TM, D_MODEL), lambda i: (i, 0)),
                _mod_spec(0, row_fn), _mod_spec(1, row_fn),
                pl.BlockSpec((1, D_MODEL), const),
                pl.BlockSpec((D_MODEL, IN_W), const, pipeline_mode=pl.Buffered(1)),
                pl.BlockSpec((1, HEAD_DIM), const),
                pl.BlockSpec((1, HEAD_DIM), const)]
    args = [x2d, mod4, mod4, norm1, w_in_bf, q_gain, k_gain]
    if rope:
        tab_spec = pl.BlockSpec((TM, HEAD_DIM), lambda i: (i % tiles_per_seq, 0))
        in_specs += [tab_spec, tab_spec]
        args += list(rope_tabs)
    in_specs += [pl.BlockSpec((G_HEADS, CHUNK, CHUNK), lambda i: (0, 0, 0)),
                 pl.BlockSpec((G_HEADS, CHUNK, G_DIM), lambda i: (0, 0, 0)),
                 pl.BlockSpec((1, GMLP_W), const)]
    args += [wsp_bf, bsp_b, gog]
    row = lambda w: pl.BlockSpec((TM, w), lambda i: (i, 0))
    out_shape = [jax.ShapeDtypeStruct((t, ATTN_W), BF16), jax.ShapeDtypeStruct((t, KV_W), BF16),
                 jax.ShapeDtypeStruct((t, KV_W), BF16), jax.ShapeDtypeStruct((t, GMLP_W), BF16)]
    out_specs = [row(ATTN_W), row(KV_W), row(KV_W), row(GMLP_W)]
    if emit_kv:
        out_shape += [jax.ShapeDtypeStruct((t, KV_W), F32)] * 2
        out_specs += [row(KV_W), row(KV_W)]
    return pl.pallas_call(
        functools.partial(_inproj_kernel, rope=rope, emit_kv=emit_kv),
        out_shape=out_shape,
        grid=(t // TM,),
        in_specs=in_specs,
        out_specs=out_specs,
        scratch_shapes=[pltpu.VMEM((TM, GMLP_W), F32)],
        compiler_params=pltpu.CompilerParams(dimension_semantics=("arbitrary",),
                                             vmem_limit_bytes=48 * MIB),
        name="inproj_rope" if rope else "inproj",
    )(*args)


def _attn_kernel(q_ref, k_ref, v_ref, gain_ref, o_ref, *, chunks):
    outs = [None] * N_HEADS
    for kvh in range(KV_HEADS):
        ks = slice(kvh * HEAD_DIM, (kvh + 1) * HEAD_DIM)
        q4 = jnp.concatenate(
            [q_ref[:, (kvh * GQA_GROUP + g) * HEAD_DIM:(kvh * GQA_GROUP + g + 1) * HEAD_DIM]
             for g in range(GQA_GROUP)], axis=0)
        m = l = acc = None
        for c0, cl in chunks:
            s = lax.dot_general(q4, k_ref[c0:c0 + cl, ks], (((1,), (1,)), ((), ())),
                                preferred_element_type=F32)
            mc = jnp.max(s, axis=-1, keepdims=True)
            m_new = mc if m is None else jnp.maximum(m, mc)
            p = jnp.exp(s - m_new)
            ps = jnp.sum(p, axis=-1, keepdims=True)
            pv = jnp.dot(p.astype(BF16), v_ref[c0:c0 + cl, ks], preferred_element_type=F32)
            if m is None:
                l, acc = ps, pv
            else:
                corr = jnp.exp(m - m_new)
                l = l * corr + ps
                acc = acc * corr + pv
            m = m_new
        o = acc / l
        for g in range(GQA_GROUP):
            outs[kvh * GQA_GROUP + g] = o[g * TQ:(g + 1) * TQ]
    ss = outs[0] * outs[0]
    for hh in range(1, N_HEADS):
        ss = ss + outs[hh] * outs[hh]
    inv = lax.rsqrt(jnp.sum(ss, axis=-1, keepdims=True) * (1.0 / ATTN_W) + EPS)
    gain = gain_ref[...]
    for hh in range(N_HEADS):
        sl = slice(hh * HEAD_DIM, (hh + 1) * HEAD_DIM)
        o_ref[:, sl] = (outs[hh] * inv * gain[:, sl]).astype(BF16)


def _attention(q2d, k3d, v3d, gain, seq_len):
    t = q2d.shape[0]
    b, l, _ = k3d.shape
    nq = seq_len // TQ
    n_chunks = -(-l // KCHUNK)
    base = (l // LANES) // n_chunks
    extra = (l // LANES) % n_chunks
    sizes = [(base + (1 if c < extra else 0)) * LANES for c in range(n_chunks)]
    chunks = tuple((sum(sizes[:c]), sizes[c]) for c in range(n_chunks))
    return pl.pallas_call(
        functools.partial(_attn_kernel, chunks=chunks),
        out_shape=jax.ShapeDtypeStruct((t, ATTN_W), BF16),
        grid=(b, nq),
        in_specs=[pl.BlockSpec((TQ, ATTN_W), lambda bi, i: (bi * nq + i, 0)),
                  pl.BlockSpec((None, l, KV_W), lambda bi, i: (bi, 0, 0)),
                  pl.BlockSpec((None, l, KV_W), lambda bi, i: (bi, 0, 0)),
                  pl.BlockSpec((1, ATTN_W), lambda bi, i: (0, 0))],
        out_specs=pl.BlockSpec((TQ, ATTN_W), lambda bi, i: (bi * nq + i, 0)),
        compiler_params=pltpu.CompilerParams(dimension_semantics=("arbitrary", "arbitrary"),
                                             vmem_limit_bytes=48 * MIB),
        name="attention",
    )(q2d, k3d, v3d, gain)


def _outproj_kernel(ap_ref, as_ref, gmp_ref, gms_ref, w_ref, xp_ref, xs_ref, g1_ref, n2_ref, sc2_ref,
                    sh2_ref, wr_ref, br_ref, x1_out, h2_out, idx_out, gate_out, rank_out, cnt_out,
                    cnt_scr, lg_scr, *, prompt_tiles, n_tiles):
    i = pl.program_id(0)
    tile = jnp.minimum(i, n_tiles - 1)
    is_p = tile < prompt_tiles

    @pl.when(i == 0)
    def _():
        cnt_scr[...] = jnp.zeros_like(cnt_scr)
        lg_scr[...] = jnp.full(lg_scr.shape, NEG, F32)

    logits = lg_scr[(i + 1) % 2]
    lane = lax.broadcasted_iota(jnp.int32, (TM, LANES), 1)
    lane_f = lane.astype(F32)
    work = logits
    vals, sels = [], []
    multi = jnp.zeros((TM, LANES), F32)
    idx_acc = jnp.zeros((TM, LANES), F32)
    for k in range(TOP_K):
        m = jnp.max(work, axis=-1, keepdims=True)
        idx = jnp.min(jnp.where(work == m, lane_f, float(LANES)), axis=-1, keepdims=True)
        sel = lane_f == idx
        vals.append(m)
        sels.append(sel)
        work = jnp.where(sel, -jnp.inf, work)
        multi = multi + sel.astype(F32)
        idx_acc = jnp.where(lane == k, idx, idx_acc)
    es = [jnp.exp(v - vals[0]) for v in vals]
    den = es[0] + es[1] + es[2] + es[3]
    gate_acc = jnp.zeros((TM, LANES), F32)
    for k in range(TOP_K):
        gate_acc = jnp.where(lane == k, es[k] / den, gate_acc)

    r_i = lax.broadcasted_iota(jnp.int32, (TM, TM), 0)
    c_i = lax.broadcasted_iota(jnp.int32, (TM, TM), 1)
    lower = (c_i < r_i).astype(BF16)
    prefix = jnp.dot(lower, multi.astype(BF16), preferred_element_type=F32) + cnt_scr[...]
    rank_acc = jnp.zeros((TM, LANES), F32)
    for k in range(TOP_K):
        r = jnp.sum(jnp.where(sels[k], prefix, 0.0), axis=-1, keepdims=True)
        rank_acc = jnp.where(lane == k, r, rank_acc)
    cnt_new = cnt_scr[...] + jnp.where(i >= 1, jnp.sum(multi, axis=0, keepdims=True), 0.0)
    cnt_scr[...] = cnt_new
    cnt_out[...] = cnt_new
    idx_out[...] = idx_acc.astype(jnp.int32)
    gate_out[...] = gate_acc
    rank_out[...] = rank_acc.astype(jnp.int32)

    a = jnp.where(is_p, ap_ref[...], as_ref[...])
    gm = jnp.where(is_p, gmp_ref[...], gms_ref[...])
    x = jnp.where(is_p, xp_ref[...], xs_ref[...])
    o = (jnp.dot(a, w_ref[0:ATTN_W, :], preferred_element_type=F32)
         + jnp.dot(gm, w_ref[ATTN_W:, :], preferred_element_type=F32))
    x1 = x + g1_ref[...] * o
    x1_out[...] = x1
    h2 = _rms(x1) * (n2_ref[...] * (1.0 + sc2_ref[...])) + sh2_ref[...]
    h2_out[...] = h2

    h_hi = h2.astype(BF16)
    h_lo = (h2 - h_hi.astype(F32)).astype(BF16)
    wr = wr_ref[...]
    w_hi = wr.astype(BF16)
    w_lo = (wr - w_hi.astype(F32)).astype(BF16)
    r2 = jnp.dot(h_hi, jnp.concatenate([w_hi, w_lo], axis=1), preferred_element_type=F32)
    lg_scr[i % 2] = (r2[:, 0:LANES] + r2[:, LANES:2 * LANES]
                     + jnp.dot(h_lo, w_hi, preferred_element_type=F32)) + br_ref[...]


def _outproj(attn_p, attn_s, gm_p, gm_s, w_out_bf, xp, xs, mod4, row_fn, norm2, wr_pad, br_pad):
    ntp = xp.shape[0] // TM
    t = xp.shape[0] + xs.shape[0]
    n_tiles = t // TM
    const = lambda i: (0, 0)
    cur = lambda i: jnp.minimum(i, n_tiles - 1)
    row = lambda w: pl.BlockSpec((TM, w), lambda i: (cur(i), 0))
    routed = lambda w: pl.BlockSpec((TM, w), lambda i: (jnp.maximum(i - 1, 0), 0))
    row_p = lambda w: pl.BlockSpec((TM, w), lambda i: (jnp.minimum(i, ntp - 1), 0))
    row_s = lambda w: pl.BlockSpec((TM, w), lambda i: (jnp.maximum(cur(i) - ntp, 0), 0))
    mod = lambda part: _mod_spec(part, lambda i: row_fn(cur(i)))
    return pl.pallas_call(
        functools.partial(_outproj_kernel, prompt_tiles=ntp, n_tiles=n_tiles),
        out_shape=[jax.ShapeDtypeStruct((t, D_MODEL), F32),
                   jax.ShapeDtypeStruct((t, D_MODEL), F32),
                   jax.ShapeDtypeStruct((t, LANES), jnp.int32),
                   jax.ShapeDtypeStruct((t, LANES), F32),
                   jax.ShapeDtypeStruct((t, LANES), jnp.int32),
                   jax.ShapeDtypeStruct((1, LANES), F32)],
        grid=(n_tiles + 1,),
        in_specs=[row_p(ATTN_W), row_s(ATTN_W), row_p(GMLP_W), row_s(GMLP_W),
                  pl.BlockSpec((D_MODEL, D_MODEL), const, pipeline_mode=pl.Buffered(1)),
                  row_p(D_MODEL), row_s(D_MODEL),
                  mod(2),
                  pl.BlockSpec((1, D_MODEL), const),
                  mod(4), mod(3),
                  pl.BlockSpec((D_MODEL, LANES), const),
                  pl.BlockSpec((1, LANES), const)],
        out_specs=[row(D_MODEL),
                   row(D_MODEL),
                   routed(LANES), routed(LANES), routed(LANES),
                   pl.BlockSpec((1, LANES), const)],
        scratch_shapes=[pltpu.VMEM((1, LANES), F32), pltpu.VMEM((2, TM, LANES), F32)],
        compiler_params=pltpu.CompilerParams(dimension_semantics=("arbitrary",),
                                             vmem_limit_bytes=48 * MIB),
        name="outproj_router",
    )(attn_p, attn_s, gm_p, gm_s, w_out_bf, xp, xs, mod4, norm2, mod4, mod4, wr_pad, br_pad)


def _scatter_rows_kernel(dest_ref, cend_ref, pend_ref, nblk_ref, h_hbm, x_hbm, ring, zbuf,
                         sem_in, sem_sc, sem_z, *, n_sub):
    i = pl.program_id(0)
    n = pl.num_programs(0)

    def in_copy(b):
        return pltpu.make_async_copy(h_hbm.at[pl.ds(_aligned(b * TC, TC), TC), :], ring.at[b % GSLOTS],
                                     sem_in.at[b % GSLOTS])

    def sc_wait(b):
        for _ in range(TOP_K):
            pltpu.make_async_copy(ring.at[b % GSLOTS], x_hbm.at[pl.ds(0, TC), :],
                                  sem_sc.at[b % GSLOTS]).wait()

    def zero_row(r):
        return pltpu.make_async_copy(zbuf.at[pl.ds(0, 1), :], x_hbm.at[pl.ds(r, 1), :], sem_z)

    def zero_block(b):
        return pltpu.make_async_copy(zbuf, x_hbm.at[pl.ds(_aligned(b * SB, SB), SB), :], sem_z)

    @pl.when(i == 0)
    def _():
        for b in range(GLAG):
            in_copy(b).start()
        zbuf[...] = jnp.zeros_like(zbuf)

        def per_expert(ex, carry):
            def start_row(r, c2):
                zero_row(r).start()
                return c2
            lax.fori_loop(cend_ref[ex], pend_ref[ex], start_row, 0)
            return carry
        lax.fori_loop(0, N_EXPERTS, per_expert, 0)

        def start_blk(b, carry):
            zero_block(b).start()
            return carry
        lax.fori_loop(nblk_ref[0], n_sub, start_blk, 0)

    in_copy(i).wait()
    slot = i % GSLOTS
    for j in range(TC):
        for k in range(TOP_K):
            r = dest_ref[(i * TC + j) * TOP_K + k]
            pltpu.make_async_copy(ring.at[slot, pl.ds(j, 1), :], x_hbm.at[pl.ds(r, 1), :],
                                  sem_sc.at[slot]).start()

    @pl.when(i >= 1)
    def _():
        sc_wait(i - 1)

    @pl.when(i + GLAG < n)
    def _():
        in_copy(i + GLAG).start()

    @pl.when(i == n - 1)
    def _():
        sc_wait(i)

        def per_expert(ex, carry):
            def wait_row(r, c2):
                zero_row(0).wait()
                return c2
            lax.fori_loop(cend_ref[ex], pend_ref[ex], wait_row, 0)
            return carry
        lax.fori_loop(0, N_EXPERTS, per_expert, 0)

        def wait_blk(b, carry):
            zero_block(0).wait()
            return carry
        lax.fori_loop(nblk_ref[0], n_sub, wait_blk, 0)


def _scatter_rows(dest_flat, cnt_end, pad_end, nblk, h_words, n_pad):
    t = h_words.shape[0]
    any_spec = pl.BlockSpec(memory_space=pl.ANY)
    return pl.pallas_call(
        functools.partial(_scatter_rows_kernel, n_sub=n_pad // SB),
        out_shape=jax.ShapeDtypeStruct((n_pad, D_MODEL), F32),
        grid_spec=pltpu.PrefetchScalarGridSpec(
            num_scalar_prefetch=4,
            grid=(t // TC,),
            in_specs=[any_spec],
            out_specs=any_spec,
            scratch_shapes=[pltpu.VMEM((GSLOTS, TC, D_MODEL), F32),
                            pltpu.VMEM((SB, D_MODEL), F32),
                            pltpu.SemaphoreType.DMA((GSLOTS,)),
                            pltpu.SemaphoreType.DMA((GSLOTS,)),
                            pltpu.SemaphoreType.DMA]),
        compiler_params=pltpu.CompilerParams(dimension_semantics=("arbitrary",),
                                             vmem_limit_bytes=32 * MIB),
        name="scatter_rows",
    )(dest_flat, cnt_end, pad_end, nblk, h_words)


def _moe_kernel(ie_ref, irow_ref, insb_ref, nblk_ref,
                x_hbm, wg_hbm, wu_hbm, wd_hbm, bgu_ref, bd_ref,
                y_hbm,
                xring, x2d, act_buf, wdb, wgus, wds, wgub, gu_scr, ystage, sem_x, sem_w, sem_y, *, n_sub):
    g = pl.program_id(0)
    nsb = insb_ref[g]
    e = ie_ref[g]
    row0 = irow_ref[g]
    nsb_next = insb_ref[g + 1]
    row0_next = irow_ref[g + 1]
    e_next = ie_ref[g + 1]

    nsb_prev = insb_ref[jnp.maximum(g - 1, 0)]

    def x_copy(src_row, c):
        return pltpu.make_async_copy(x_hbm.at[pl.ds(_aligned(src_row + c * SB, SB), SB), :],
                                     xring.at[c % 2], sem_x.at[c % 2])

    def land(src_row, c, n_total):
        x_copy(src_row, c).wait()
        x2d[pl.ds(_aligned(c * SB, SB), SB), :] = xring[c % 2].astype(BF16)

        @pl.when(c + 2 < n_total)
        def _():
            x_copy(src_row, c + 2).start()

    def issue_first(src_row, n_total):
        for c in range(2):
            pl.when(c < n_total)(lambda c=c: x_copy(src_row, c).start())

    def w_copies(ex, f, slot):
        cols = pl.ds(_aligned(f * TF, TF), TF)
        return (pltpu.make_async_copy(wg_hbm.at[ex, :, cols], wgus.at[slot, 0], sem_w.at[slot, 0]),
                pltpu.make_async_copy(wu_hbm.at[ex, :, cols], wgus.at[slot, 1], sem_w.at[slot, 1]),
                pltpu.make_async_copy(wd_hbm.at[ex, cols, :], wds.at[slot], sem_w.at[slot, 2]))

    def y_copy(sb, slot):
        return pltpu.make_async_copy(ystage.at[slot],
                                     y_hbm.at[pl.ds(_aligned(row0 + sb * SB, SB), SB), :], sem_y.at[slot])

    def catch_up(c, carry):
        land(row0, c, nsb)
        return carry
    lax.fori_loop(jnp.minimum(nsb_prev, nsb), nsb, catch_up, 0)

    @pl.when(nsb == 0)
    def _():
        issue_first(row0_next, nsb_next)

    @pl.when(nsb > 0)
    def _():

        @pl.when(g == 1)
        def _():
            for cp in w_copies(e, 0, 0):
                cp.start()
            gu_scr[...] = jnp.zeros_like(gu_scr)

        def swiglu(gu, f, rows):
            gt = jnp.minimum(gu[:, 0:TF], SWIGLU_LIMIT)
            up = jnp.clip(gu[:, TF:2 * TF], -SWIGLU_LIMIT, SWIGLU_LIMIT)
            act = (up + 1.0) * (gt * jax.nn.sigmoid(SWIGLU_ALPHA * gt))
            act_buf[f, rows, :] = act.astype(BF16)

        def gate_up_tile(f, carry):
            slot = f % 2

            @pl.when(f + 1 < NF)
            def _():
                for cp in w_copies(e, f + 1, 1 - slot):
                    cp.start()
            for cp in w_copies(e, f, slot):
                cp.wait()
            wgub[:, 0:TF] = wgus[slot, 0].astype(BF16)
            wgub[:, TF:2 * TF] = wgus[slot, 1].astype(BF16)
            wdb[pl.ds(_aligned(f * TF, TF), TF), :] = wds[slot].astype(BF16)
            b_gu = bgu_ref[pl.ds(e * NF + f, 1), :]
            f_prev = jnp.maximum(f - 1, 0)
            b_prev = bgu_ref[pl.ds(e * NF + f_prev, 1), :]

            def gate_up(row_start, m):
                rows = pl.ds(row_start, m)
                swiglu(jnp.dot(x2d[rows, :], wgub[...], preferred_element_type=F32) + b_gu, f, rows)

            for big in BIG_NSB:
                @pl.when(nsb == big)
                def _(big=big):
                    half = big * SB // 2
                    rows_a, rows_b = pl.ds(0, half), pl.ds(half, half)
                    swiglu(gu_scr[0:half, :] + b_prev, f_prev, rows_b)
                    gate_up(0, half)
                    gu_scr[0:half, :] = jnp.dot(x2d[rows_b, :], wgub[...], preferred_element_type=F32)

            @pl.when(functools.reduce(lambda a, b: a & b, [nsb != big for big in BIG_NSB]))
            def _():
                def one(sb, c2):
                    gate_up(_aligned(sb * SB, SB), SB)
                    return c2
                lax.fori_loop(0, nsb, one, 0)
            return carry
        lax.fori_loop(0, NF, gate_up_tile, 0)

        @pl.when(nsb_next > 0)
        def _():
            for cp in w_copies(e_next, 0, 0):
                cp.start()

        for big in BIG_NSB:
            @pl.when(nsb == big)
            def _(big=big):
                half = big * SB // 2
                swiglu(gu_scr[0:half, :] + bgu_ref[pl.ds(e * NF + NF - 1, 1), :], NF - 1,
                       pl.ds(half, half))

        issue_first(row0_next, nsb_next)

        def down(sb, carry):
            yslot = sb % 2

            @pl.when(sb >= 2)
            def _():
                y_copy(sb - 2, yslot).wait()

            @pl.when(sb < nsb_next)
            def _():
                land(row0_next, sb, nsb_next)
            rows = pl.ds(_aligned(sb * SB, SB), SB)
            a = jnp.concatenate([act_buf[f, rows, :] for f in range(NF)], axis=1)
            ystage[yslot] = jnp.dot(a, wdb[...], preferred_element_type=F32) + bd_ref[pl.ds(e, 1), :]
            y_copy(sb, yslot).start()
            return carry
        lax.fori_loop(0, nsb, down, 0)
        y_copy(nsb - 1, (nsb - 1) % 2).wait()

        @pl.when(nsb >= 2)
        def _():
            y_copy(nsb - 2, nsb % 2).wait()

    @pl.when(g == pl.num_programs(0) - 1)
    def _():
        ystage[0] = jnp.zeros((SB, D_MODEL), F32)

        def fill(b):
            return pltpu.make_async_copy(ystage.at[0], y_hbm.at[pl.ds(_aligned(b * SB, SB), SB), :],
                                         sem_y.at[0])

        def start_fill(b, carry):
            fill(b).start()
            return carry

        def wait_fill(b, carry):
            fill(b).wait()
            return carry
        lax.fori_loop(nblk_ref[0], n_sub, start_fill, 0)
        lax.fori_loop(nblk_ref[0], n_sub, wait_fill, 0)


def _moe(item_e, item_row, item_nsb, nblk, x_sorted, w_gate, w_up, w_down, b_gate, b_up, b_down,
         n_items):
    any_spec = pl.BlockSpec(memory_space=pl.ANY)
    b_gu = jnp.concatenate([b_gate.reshape(N_EXPERTS * NF, TF), b_up.reshape(N_EXPERTS * NF, TF)], axis=1)
    grid_spec = pltpu.PrefetchScalarGridSpec(
        num_scalar_prefetch=4,
        grid=(n_items + 1,),
        in_specs=[any_spec, any_spec, any_spec, any_spec,
                  pl.BlockSpec((N_EXPERTS * NF, 2 * TF), lambda i, *_: (0, 0)),
                  pl.BlockSpec((N_EXPERTS, D_MODEL), lambda i, *_: (0, 0))],
        out_specs=any_spec,
        scratch_shapes=[pltpu.VMEM((2, SB, D_MODEL), F32),
                        pltpu.VMEM((RB, D_MODEL), BF16),
                        pltpu.VMEM((NF, RB, TF), BF16),
                        pltpu.VMEM((D_FF, D_MODEL), BF16),
                        pltpu.VMEM((2, 2, D_MODEL, TF), F32),
                        pltpu.VMEM((2, TF, D_MODEL), F32),
                        pltpu.VMEM((D_MODEL, 2 * TF), BF16),
                        pltpu.VMEM((RB // 2, 2 * TF), F32),
                        pltpu.VMEM((2, SB, D_MODEL), F32),
                        pltpu.SemaphoreType.DMA((2,)),
                        pltpu.SemaphoreType.DMA((2, 3)),
                        pltpu.SemaphoreType.DMA((2,))],
    )
    return pl.pallas_call(
        functools.partial(_moe_kernel, n_sub=x_sorted.shape[0] // SB),
        out_shape=jax.ShapeDtypeStruct((x_sorted.shape[0], D_MODEL), F32),
        grid_spec=grid_spec,
        compiler_params=pltpu.CompilerParams(dimension_semantics=("arbitrary",),
                                             vmem_limit_bytes=56 * MIB),
        name="moe_experts",
    )(item_e, item_row, item_nsb, nblk, x_sorted, w_gate, w_up, w_down, b_gu, b_down)


def _combine_kernel(dest_ref, y_hbm, gate_ref, x1_ref, g2_ref, nf_ref, o_ref, ybuf, sem, *, tok_off):
    i = pl.program_id(0)
    n = pl.num_programs(0)

    def issue(b):
        slot = b % GSLOTS
        for j in range(TC):
            for k in range(TOP_K):
                r = dest_ref[(tok_off + b * TC + j) * TOP_K + k]
                pltpu.make_async_copy(y_hbm.at[pl.ds(r, 1), :], ybuf.at[slot, k, pl.ds(j, 1), :],
                                      sem.at[slot]).start()

    @pl.when(i == 0)
    def _():
        for b in range(GLAG):
            issue(b)

    pl.when(i + GLAG < n)(lambda: issue(i + GLAG))

    slot = i % GSLOTS
    for k in range(TOP_K):
        pltpu.make_async_copy(y_hbm.at[pl.ds(0, TC), :], ybuf.at[slot, k], sem.at[slot]).wait()
    gate = gate_ref[...]
    f = gate[:, 0:1] * ybuf[slot, 0]
    for k in range(1, TOP_K):
        f = f + gate[:, k:k + 1] * ybuf[slot, k]
    x2 = x1_ref[...] + g2_ref[...] * f
    o_ref[...] = _rms(x2) * nf_ref[...]


def _combine(dest, y_sorted, gates, x1, mod4, row_fn, norm_f, tok_off, t):
    off = tok_off // TC
    row_in = lambda w: pl.BlockSpec((TC, w), lambda i, *_: (off + i, 0))
    return pl.pallas_call(
        functools.partial(_combine_kernel, tok_off=tok_off),
        out_shape=jax.ShapeDtypeStruct((t, D_MODEL), F32),
        grid_spec=pltpu.PrefetchScalarGridSpec(
            num_scalar_prefetch=1,
            grid=(t // TC,),
            in_specs=[pl.BlockSpec(memory_space=pl.ANY), row_in(LANES), row_in(D_MODEL),
                      pl.BlockSpec((None, None, 1, D_MODEL),
                                   lambda i, *_: (row_fn(i * TC // TM), 5, 0, 0)),
                      pl.BlockSpec((1, D_MODEL), lambda i, *_: (0, 0))],
            out_specs=pl.BlockSpec((TC, D_MODEL), lambda i, *_: (i, 0)),
            scratch_shapes=[pltpu.VMEM((GSLOTS, TOP_K, TC, D_MODEL), F32),
                            pltpu.SemaphoreType.DMA((GSLOTS,))]),
        compiler_params=pltpu.CompilerParams(dimension_semantics=("arbitrary",),
                                             vmem_limit_bytes=40 * MIB),
        name="combine_norm",
    )(dest, y_sorted, gates, x1, mod4, norm_f)


def _rope_tables(n_tokens):
    n_rows = n_tokens // GRID_W
    row = jnp.repeat(jnp.arange(n_rows, dtype=F32), GRID_W)
    col = jnp.tile(jnp.arange(GRID_W, dtype=F32), n_rows)
    half = AXIS_DIM // 2
    inv_freq = ROPE_THETA ** (-jnp.arange(half, dtype=F32) / half)
    ang_r = row[:, None] * inv_freq[None, :]
    ang_c = col[:, None] * inv_freq[None, :]
    cos_t = jnp.concatenate([jnp.cos(ang_r)] * 2 + [jnp.cos(ang_c)] * 2, axis=-1)
    sin_t = jnp.concatenate([-jnp.sin(ang_r), jnp.sin(ang_r), -jnp.sin(ang_c), jnp.sin(ang_c)], axis=-1)
    return cos_t, sin_t


def kernel(x_prompt, x_sample, cache_k, cache_v, c, c_ctx, w_mod, b_mod, norm1, w_in, q_gain, k_gain,
           w_sp, b_sp, attn_out_gain, gmlp_out_gain, w_out, norm2, w_router, b_router,
           w_gate, b_gate, w_up, b_up, w_down, b_down, norm_f):
    bp, sp_len, _ = x_prompt.shape
    bs, ss_len, _ = x_sample.shape
    depth = w_mod.shape[0]
    n_p = bp * sp_len
    n_s = bs * ss_len
    t_total = n_p + n_s
    n_assign = t_total * TOP_K
    n_sub = (n_assign + N_EXPERTS * (SB - 1)) // SB + 1
    n_pad = n_sub * SB
    n_items = N_EXPERTS + -(-n_pad // RB)

    xp = x_prompt.reshape(n_p, D_MODEL)
    xs = x_sample.reshape(n_s, D_MODEL)
    tiles_per_sample = ss_len // TM
    row_p = lambda i: 0
    row_s = lambda i: 1 + i // tiles_per_sample
    row_all = lambda i: jnp.where(i < n_p // TM, 0, 1 + (i - n_p // TM) // tiles_per_sample)
    rope_tabs = _rope_tables(ss_len)
    cvec8 = jnp.zeros((8, D_MODEL), F32).at[0].set(c_ctx).at[1:1 + bs].set(c)
    new_k, new_v = [], []

    for l in range(depth):
        mod4 = _modulation(cvec8, w_mod[l], b_mod[l]).reshape(8, N_MOD, 1, D_MODEL)
        w_in_bf = w_in[l].astype(BF16)
        w_out_bf = w_out[l].astype(BF16)
        wsp_bf = w_sp[l].astype(BF16)
        bsp_b = jnp.broadcast_to(b_sp[l][:, :, None], (G_HEADS, CHUNK, G_DIM))
        n1 = norm1[l].reshape(1, D_MODEL)
        n2 = norm2[l].reshape(1, D_MODEL)
        qg = q_gain[l].reshape(1, HEAD_DIM)
        kg = k_gain[l].reshape(1, HEAD_DIM)
        gog = gmlp_out_gain[l].reshape(1, GMLP_W)
        aog = attn_out_gain[l].reshape(1, ATTN_W)
        wr_pad = jnp.zeros((D_MODEL, LANES), F32).at[:, :N_EXPERTS].set(w_router[l])
        br_pad = jnp.full((1, LANES), NEG, F32).at[0, :N_EXPERTS].set(b_router[l])

        qp, kp, vp, gmp, kpf, vpf = _inproj(xp, mod4, row_p, n1, w_in_bf, qg, kg, None, wsp_bf, bsp_b,
                                            gog, sp_len, True)
        qs, ks, vs, gms = _inproj(xs, mod4, row_s, n1, w_in_bf, qg, kg, rope_tabs, wsp_bf, bsp_b,
                                  gog, ss_len, False)
        new_k.append(kpf.reshape(bp, sp_len, KV_HEADS, HEAD_DIM))
        new_v.append(vpf.reshape(bp, sp_len, KV_HEADS, HEAD_DIM))

        attn_p = _attention(qp, kp.reshape(bp, sp_len, KV_W), vp.reshape(bp, sp_len, KV_W), aog, sp_len)
        k_all = jnp.concatenate([cache_k[:, l].reshape(bs, -1, KV_W).astype(BF16),
                                 ks.reshape(bs, ss_len, KV_W)], axis=1)
        v_all = jnp.concatenate([cache_v[:, l].reshape(bs, -1, KV_W).astype(BF16),
                                 vs.reshape(bs, ss_len, KV_W)], axis=1)
        attn_s = _attention(qs, k_all, v_all, aog, ss_len)

        x1, h2, idx_l, gates, rank_l, cnt = _outproj(attn_p, attn_s, gmp, gms, w_out_bf, xp, xs, mod4,
                                                     row_all, n2, wr_pad, br_pad)

        idx = idx_l[:, :TOP_K]
        rank = rank_l[:, :TOP_K]
        counts = cnt[0, :N_EXPERTS].astype(jnp.int32)
        padded = ((counts + SB - 1) // SB) * SB
        pad_end = jnp.cumsum(padded)
        pad_start = pad_end - padded
        dest = (pad_start[idx] + rank).astype(jnp.int32)
        dest_flat = dest.reshape(-1)
        items_e = (padded + RB - 1) // RB
        items_end = jnp.cumsum(items_e)
        j = jnp.arange(n_items, dtype=jnp.int32)
        item_e = jnp.minimum(jnp.sum((items_end[None, :] <= j[:, None]).astype(jnp.int32), axis=1),
                             N_EXPERTS - 1)
        within = j - (items_end - items_e)[item_e]
        item_row = pad_start[item_e] + within * RB
        item_rows = jnp.clip(padded[item_e] - within * RB, 0, RB)
        item_nsb = jnp.where(j < items_end[-1], item_rows // SB, 0).astype(jnp.int32)
        item_row = jnp.where(item_nsb > 0, item_row, 0).astype(jnp.int32)

        nblk = (pad_end[-1:] // SB).astype(jnp.int32)
        x_sorted = _scatter_rows(dest_flat, (pad_start + counts).astype(jnp.int32),
                                 pad_end.astype(jnp.int32), nblk, h2, n_pad)
        framed = lambda a: jnp.pad(a.astype(jnp.int32), (1, 1))
        y_sorted = _moe(framed(item_e), framed(item_row), framed(item_nsb), nblk, x_sorted,
                        w_gate[l], w_up[l], w_down[l], b_gate[l], b_up[l], b_down[l], n_items)

        nf = norm_f.reshape(1, D_MODEL)
        is_last = l == depth - 1
        assert is_last, "only DEPTH == 1 is supported"
        dest_flat = dest.reshape(-1)
        y_prompt = _combine(dest_flat, y_sorted, gates, x1, mod4, row_p, nf, 0, n_p)
        y_sample = _combine(dest_flat, y_sorted, gates, x1, mod4, row_s, nf, n_p, n_s)

    return (y_prompt.reshape(x_prompt.shape), y_sample.reshape(x_sample.shape),
            jnp.stack(new_k, axis=1), jnp.stack(new_v, axis=1))
```

```python
import functools

import jax
import jax.numpy as jnp
from jax import lax
from jax.experimental import pallas as pl
from jax.experimental.pallas import tpu as pltpu

F32 = jnp.float32
BF16 = jnp.bfloat16

D_MODEL = 2048
GRID_W = 64
N_HEADS = 8
KV_HEADS = 2
HEAD_DIM = 128
GQA_GROUP = N_HEADS // KV_HEADS
ATTN_W = N_HEADS * HEAD_DIM
KV_W = KV_HEADS * HEAD_DIM
G_HEADS = 8
G_DIM = (D_MODEL - ATTN_W) // G_HEADS
GMLP_W = G_HEADS * G_DIM
IN_W = ATTN_W + 2 * KV_W + 2 * GMLP_W
CHUNK = 128
ROPE_THETA = 10000.0
AXIS_DIM = HEAD_DIM // 2
N_EXPERTS = 32
TOP_K = 4
D_FF = D_MODEL
SWIGLU_ALPHA = 1.702
SWIGLU_LIMIT = 7.0
N_MOD = 6
EPS = 1e-6

LANES = 128
MIB = 1024 * 1024

TM = 256
TQ = 256
KCHUNK = 1152
TC = 128
MOD_TN = 1024
SB = 256
RB = 1792
TF = 256
NF = D_FF // TF
BIG_NSB = (RB // SB, RB // SB - 1)
GLAG = 2
GSLOTS = GLAG + 1
NEG = -1e30


def _rms(x):
    return x * lax.rsqrt(jnp.mean(x * x, axis=-1, keepdims=True) + EPS)


def _aligned(x, m):
    if isinstance(x, int):
        assert x % m == 0
        return x
    return pl.multiple_of(x, m)


def _mod_kernel(c_ref, w_ref, b_ref, o_ref):
    c = c_ref[...]
    a = c * jax.nn.sigmoid(c)
    o_ref[...] = jnp.dot(a.astype(BF16), w_ref[...].astype(BF16),
                         preferred_element_type=F32) + b_ref[...]


def _modulation(cvec8, w_mod, b_mod):
    n = w_mod.shape[1]
    return pl.pallas_call(
        _mod_kernel,
        out_shape=jax.ShapeDtypeStruct((8, n), F32),
        grid=(n // MOD_TN,),
        in_specs=[pl.BlockSpec((8, D_MODEL), lambda j: (0, 0)),
                  pl.BlockSpec((D_MODEL, MOD_TN), lambda j: (0, j)),
                  pl.BlockSpec((1, MOD_TN), lambda j: (0, j))],
        out_specs=pl.BlockSpec((8, MOD_TN), lambda j: (0, j)),
        compiler_params=pltpu.CompilerParams(dimension_semantics=("arbitrary",),
                                             vmem_limit_bytes=40 * MIB),
        name="modulation",
    )(cvec8, w_mod, b_mod.reshape(1, n))


def _mod_spec(part, row_fn):
    return pl.BlockSpec((None, None, 1, D_MODEL), lambda i, *_: (row_fn(i), part, 0, 0))


def _inproj_kernel(*refs, rope, emit_kv):
    it = iter(refs)
    x_ref, shift_ref, scale_ref, norm_ref, w_ref, qg_ref, kg_ref = (next(it) for _ in range(7))
    cos_ref = sin_ref = None
    if rope:
        cos_ref, sin_ref = next(it), next(it)
    wsp_ref, bsp_ref, gog_ref = next(it), next(it), next(it)
    q_out, k_out, v_out, gm_out = (next(it) for _ in range(4))
    kf_out = vf_out = None
    if emit_kv:
        kf_out, vf_out = next(it), next(it)
    gm_scr = next(it)

    x = x_ref[...]
    h = _rms(x) * (norm_ref[...] * (1.0 + scale_ref[...])) + shift_ref[...]
    z = jnp.dot(h.astype(BF16), w_ref[...], preferred_element_type=F32)

    if rope:
        cosv = cos_ref[...]
        sinv = sin_ref[...]
        lane = lax.broadcasted_iota(jnp.int32, (TM, HEAD_DIM), 1)
        first_half = (lane % AXIS_DIM) < (AXIS_DIM // 2)

    def head(zc, gain):
        n = _rms(zc) * gain
        if rope:
            partner = jnp.where(first_half, pltpu.roll(n, HEAD_DIM - AXIS_DIM // 2, 1),
                                pltpu.roll(n, AXIS_DIM // 2, 1))
            n = n * cosv + partner * sinv
        return n

    qg = qg_ref[...]
    kg = kg_ref[...]
    sm_scale = HEAD_DIM ** -0.5
    for hh in range(N_HEADS):
        sl = slice(hh * HEAD_DIM, (hh + 1) * HEAD_DIM)
        q_out[:, sl] = (head(z[:, sl], qg) * sm_scale).astype(BF16)
    for hh in range(KV_HEADS):
        sl = slice(hh * HEAD_DIM, (hh + 1) * HEAD_DIM)
        kh = head(z[:, ATTN_W + hh * HEAD_DIM:ATTN_W + (hh + 1) * HEAD_DIM], kg)
        k_out[:, sl] = kh.astype(BF16)
        if emit_kv:
            kf_out[:, sl] = kh
    v = z[:, ATTN_W + KV_W:ATTN_W + 2 * KV_W]
    v_out[...] = v.astype(BF16)
    if emit_kv:
        vf_out[...] = v

    u0 = ATTN_W + 2 * KV_W
    g0 = u0 + GMLP_W
    for hh in range(G_HEADS):
        sl = slice(hh * G_DIM, (hh + 1) * G_DIM)
        u = jax.nn.gelu(z[:, u0 + hh * G_DIM:u0 + (hh + 1) * G_DIM])
        g = _rms(jax.nn.gelu(z[:, g0 + hh * G_DIM:g0 + (hh + 1) * G_DIM])).astype(BF16)
        w_h = wsp_ref[hh]
        b_h = bsp_ref[hh]
        for c in range(TM // CHUNK):
            rs = slice(c * CHUNK, (c + 1) * CHUNK)
            sp = jnp.dot(w_h, g[rs], preferred_element_type=F32) + b_h
            gm_scr[rs, sl] = u[rs] * sp
    gm_out[...] = (_rms(gm_scr[...]) * gog_ref[...]).astype(BF16)


def _inproj(x2d, mod4, row_fn, norm1, w_in_bf, q_gain, k_gain, rope_tabs, wsp_bf, bsp_b, gog,
            seq_len, emit_kv):
    t = x2d.shape[0]
    rope = rope_tabs is not None
    tiles_per_seq = seq_len // TM
    const = lambda i: (0, 0)
    in_specs = [pl.BlockSpec((TM, D_MODEL), lambda i: (i, 0)),
                _mod_spec(0, row_fn), _mod_spec(1, row_fn),
                pl.BlockSpec((1, D_MODEL), const),
                pl.BlockSpec((D_MODEL, IN_W), const, pipeline_mode=pl.Buffered(1)),
                pl.BlockSpec((1, HEAD_DIM), const),
                pl.BlockSpec((1, HEAD_DIM), const)]
    args = [x2d, mod4, mod4, norm1, w_in_bf, q_gain, k_gain]
    if rope:
        tab_spec = pl.BlockSpec((TM, HEAD_DIM), lambda i: (i % tiles_per_seq, 0))
        in_specs += [tab_spec, tab_spec]
        args += list(rope_tabs)
    in_specs += [pl.BlockSpec((G_HEADS, CHUNK, CHUNK), lambda i: (0, 0, 0)),
                 pl.BlockSpec((G_HEADS, CHUNK, G_DIM), lambda i: (0, 0, 0)),
                 pl.BlockSpec((1, GMLP_W), const)]
    args += [wsp_bf, bsp_b, gog]
    row = lambda w: pl.BlockSpec((TM, w), lambda i: (i, 0))
    out_shape = [jax.ShapeDtypeStruct((t, ATTN_W), BF16), jax.ShapeDtypeStruct((t, KV_W), BF16),
                 jax.ShapeDtypeStruct((t, KV_W), BF16), jax.ShapeDtypeStruct((t, GMLP_W), BF16)]
    out_specs = [row(ATTN_W), row(KV_W), row(KV_W), row(GMLP_W)]
    if emit_kv:
        out_shape += [jax.ShapeDtypeStruct((t, KV_W), F32)] * 2
        out_specs += [row(KV_W), row(KV_W)]
    return pl.pallas_call(
        functools.partial(_inproj_kernel, rope=rope, emit_kv=emit_kv),
        out_shape=out_shape,
        grid=(t // TM,),
        in_specs=in_specs,
        out_specs=out_specs,
        scratch_shapes=[pltpu.VMEM((TM, GMLP_W), F32)],
        compiler_params=pltpu.CompilerParams(dimension_semantics=("arbitrary",),
                                             vmem_limit_bytes=48 * MIB),
        name="inproj_rope" if rope else "inproj",
    )(*args)


def _attn_kernel(q_ref, k_ref, v_ref, gain_ref, o_ref, *, chunks):
    outs = [None] * N_HEADS
    for kvh in range(KV_HEADS):
        ks = slice(kvh * HEAD_DIM, (kvh + 1) * HEAD_DIM)
        q4 = jnp.concatenate(
            [q_ref[:, (kvh * GQA_GROUP + g) * HEAD_DIM:(kvh * GQA_GROUP + g + 1) * HEAD_DIM]
             for g in range(GQA_GROUP)], axis=0)
        m = l = acc = None
        for c0, cl in chunks:
            s = lax.dot_general(q4, k_ref[c0:c0 + cl, ks], (((1,), (1,)), ((), ())),
                                preferred_element_type=F32)
            mc = jnp.max(s, axis=-1, keepdims=True)
            m_new = mc if m is None else jnp.maximum(m, mc)
            p = jnp.exp(s - m_new)
            ps = jnp.sum(p, axis=-1, keepdims=True)
            pv = jnp.dot(p.astype(BF16), v_ref[c0:c0 + cl, ks], preferred_element_type=F32)
            if m is None:
                l, acc = ps, pv
            else:
                corr = jnp.exp(m - m_new)
                l = l * corr + ps
                acc = acc * corr + pv
            m = m_new
        o = acc / l
        for g in range(GQA_GROUP):
            outs[kvh * GQA_GROUP + g] = o[g * TQ:(g + 1) * TQ]
    ss = outs[0] * outs[0]
    for hh in range(1, N_HEADS):
        ss = ss + outs[hh] * outs[hh]
    inv = lax.rsqrt(jnp.sum(ss, axis=-1, keepdims=True) * (1.0 / ATTN_W) + EPS)
    gain = gain_ref[...]
    for hh in range(N_HEADS):
        sl = slice(hh * HEAD_DIM, (hh + 1) * HEAD_DIM)
        o_ref[:, sl] = (outs[hh] * inv * gain[:, sl]).astype(BF16)


def _attention(q2d, k3d, v3d, gain, seq_len):
    t = q2d.shape[0]
    b, l, _ = k3d.shape
    nq = seq_len // TQ
    n_chunks = -(-l // KCHUNK)
    base = (l // LANES) // n_chunks
    extra = (l // LANES) % n_chunks
    sizes = [(base + (1 if c < extra else 0)) * LANES for c in range(n_chunks)]
    chunks = tuple((sum(sizes[:c]), sizes[c]) for c in range(n_chunks))
    return pl.pallas_call(
        functools.partial(_attn_kernel, chunks=chunks),
        out_shape=jax.ShapeDtypeStruct((t, ATTN_W), BF16),
        grid=(b, nq),
        in_specs=[pl.BlockSpec((TQ, ATTN_W), lambda bi, i: (bi * nq + i, 0)),
                  pl.BlockSpec((None, l, KV_W), lambda bi, i: (bi, 0, 0)),
                  pl.BlockSpec((None, l, KV_W), lambda bi, i: (bi, 0, 0)),
                  pl.BlockSpec((1, ATTN_W), lambda bi, i: (0, 0))],
        out_specs=pl.BlockSpec((TQ, ATTN_W), lambda bi, i: (bi * nq + i, 0)),
        compiler_params=pltpu.CompilerParams(dimension_semantics=("arbitrary", "arbitrary"),
                                             vmem_limit_bytes=48 * MIB),
        name="attention",
    )(q2d, k3d, v3d, gain)


def _outproj_kernel(ap_ref, as_ref, gmp_ref, gms_ref, w_ref, xp_ref, xs_ref, g1_ref, n2_ref, sc2_ref,
                    sh2_ref, wr_ref, br_ref, x1_out, h2_out, idx_out, gate_out, rank_out, cnt_out,
                    cnt_scr, lg_scr, *, prompt_tiles, n_tiles):
    i = pl.program_id(0)
    tile = jnp.minimum(i, n_tiles - 1)
    is_p = tile < prompt_tiles

    @pl.when(i == 0)
    def _():
        cnt_scr[...] = jnp.zeros_like(cnt_scr)
        lg_scr[...] = jnp.full(lg_scr.shape, NEG, F32)

    logits = lg_scr[(i + 1) % 2]
    lane = lax.broadcasted_iota(jnp.int32, (TM, LANES), 1)
    lane_f = lane.astype(F32)
    work = logits
    vals, sels = [], []
    multi = jnp.zeros((TM, LANES), F32)
    idx_acc = jnp.zeros((TM, LANES), F32)
    for k in range(TOP_K):
        m = jnp.max(work, axis=-1, keepdims=True)
        idx = jnp.min(jnp.where(work == m, lane_f, float(LANES)), axis=-1, keepdims=True)
        sel = lane_f == idx
        vals.append(m)
        sels.append(sel)
        work = jnp.where(sel, -jnp.inf, work)
        multi = multi + sel.astype(F32)
        idx_acc = jnp.where(lane == k, idx, idx_acc)
    es = [jnp.exp(v - vals[0]) for v in vals]
    den = es[0] + es[1] + es[2] + es[3]
    gate_acc = jnp.zeros((TM, LANES), F32)
    for k in range(TOP_K):
        gate_acc = jnp.where(lane == k, es[k] / den, gate_acc)

    r_i = lax.broadcasted_iota(jnp.int32, (TM, TM), 0)
    c_i = lax.broadcasted_iota(jnp.int32, (TM, TM), 1)
    lower = (c_i < r_i).astype(BF16)
    prefix = jnp.dot(lower, multi.astype(BF16), preferred_element_type=F32) + cnt_scr[...]
    rank_acc = jnp.zeros((TM, LANES), F32)
    for k in range(TOP_K):
        r = jnp.sum(jnp.where(sels[k], prefix, 0.0), axis=-1, keepdims=True)
        rank_acc = jnp.where(lane == k, r, rank_acc)
    cnt_new = cnt_scr[...] + jnp.where(i >= 1, jnp.sum(multi, axis=0, keepdims=True), 0.0)
    cnt_scr[...] = cnt_new
    cnt_out[...] = cnt_new
    idx_out[...] = idx_acc.astype(jnp.int32)
    gate_out[...] = gate_acc
    rank_out[...] = rank_acc.astype(jnp.int32)

    a = jnp.where(is_p, ap_ref[...], as_ref[...])
    gm = jnp.where(is_p, gmp_ref[...], gms_ref[...])
    x = jnp.where(is_p, xp_ref[...], xs_ref[...])
    o = (jnp.dot(a, w_ref[0:ATTN_W, :], preferred_element_type=F32)
         + jnp.dot(gm, w_ref[ATTN_W:, :], preferred_element_type=F32))
    x1 = x + g1_ref[...] * o
    x1_out[...] = x1
    h2 = _rms(x1) * (n2_ref[...] * (1.0 + sc2_ref[...])) + sh2_ref[...]
    h2_out[...] = h2

    h_hi = h2.astype(BF16)
    h_lo = (h2 - h_hi.astype(F32)).astype(BF16)
    wr = wr_ref[...]
    w_hi = wr.astype(BF16)
    w_lo = (wr - w_hi.astype(F32)).astype(BF16)
    r2 = jnp.dot(h_hi, jnp.concatenate([w_hi, w_lo], axis=1), preferred_element_type=F32)
    lg_scr[i % 2] = (r2[:, 0:LANES] + r2[:, LANES:2 * LANES]
                     + jnp.dot(h_lo, w_hi, preferred_element_type=F32)) + br_ref[...]


def _outproj(attn_p, attn_s, gm_p, gm_s, w_out_bf, xp, xs, mod4, row_fn, norm2, wr_pad, br_pad):
    ntp = xp.shape[0] // TM
    t = xp.shape[0] + xs.shape[0]
    n_tiles = t // TM
    const = lambda i: (0, 0)
    cur = lambda i: jnp.minimum(i, n_tiles - 1)
    row = lambda w: pl.BlockSpec((TM, w), lambda i: (cur(i), 0))
    routed = lambda w: pl.BlockSpec((TM, w), lambda i: (jnp.maximum(i - 1, 0), 0))
    row_p = lambda w: pl.BlockSpec((TM, w), lambda i: (jnp.minimum(i, ntp - 1), 0))
    row_s = lambda w: pl.BlockSpec((TM, w), lambda i: (jnp.maximum(cur(i) - ntp, 0), 0))
    mod = lambda part: _mod_spec(part, lambda i: row_fn(cur(i)))
    return pl.pallas_call(
        functools.partial(_outproj_kernel, prompt_tiles=ntp, n_tiles=n_tiles),
        out_shape=[jax.ShapeDtypeStruct((t, D_MODEL), F32),
                   jax.ShapeDtypeStruct((t, D_MODEL), F32),
                   jax.ShapeDtypeStruct((t, LANES), jnp.int32),
                   jax.ShapeDtypeStruct((t, LANES), F32),
                   jax.ShapeDtypeStruct((t, LANES), jnp.int32),
                   jax.ShapeDtypeStruct((1, LANES), F32)],
        grid=(n_tiles + 1,),
        in_specs=[row_p(ATTN_W), row_s(ATTN_W), row_p(GMLP_W), row_s(GMLP_W),
                  pl.BlockSpec((D_MODEL, D_MODEL), const, pipeline_mode=pl.Buffered(1)),
                  row_p(D_MODEL), row_s(D_MODEL),
                  mod(2),
                  pl.BlockSpec((1, D_MODEL), const),
                  mod(4), mod(3),
                  pl.BlockSpec((D_MODEL, LANES), const),
                  pl.BlockSpec((1, LANES), const)],
        out_specs=[row(D_MODEL),
                   row(D_MODEL),
                   routed(LANES), routed(LANES), routed(LANES),
                   pl.BlockSpec((1, LANES), const)],
        scratch_shapes=[pltpu.VMEM((1, LANES), F32), pltpu.VMEM((2, TM, LANES), F32)],
        compiler_params=pltpu.CompilerParams(dimension_semantics=("arbitrary",),
                                             vmem_limit_bytes=48 * MIB),
        name="outproj_router",
    )(attn_p, attn_s, gm_p, gm_s, w_out_bf, xp, xs, mod4, norm2, mod4, mod4, wr_pad, br_pad)


def _scatter_rows_kernel(dest_ref, cend_ref, pend_ref, nblk_ref, h_hbm, x_hbm, ring, zbuf,
                         sem_in, sem_sc, sem_z, *, n_sub):
    i = pl.program_id(0)
    n = pl.num_programs(0)

    def in_copy(b):
        return pltpu.make_async_copy(h_hbm.at[pl.ds(_aligned(b * TC, TC), TC), :], ring.at[b % GSLOTS],
                                     sem_in.at[b % GSLOTS])

    def sc_wait(b):
        for _ in range(TOP_K):
            pltpu.make_async_copy(ring.at[b % GSLOTS], x_hbm.at[pl.ds(0, TC), :],
                                  sem_sc.at[b % GSLOTS]).wait()

    def zero_row(r):
        return pltpu.make_async_copy(zbuf.at[pl.ds(0, 1), :], x_hbm.at[pl.ds(r, 1), :], sem_z)

    def zero_block(b):
        return pltpu.make_async_copy(zbuf, x_hbm.at[pl.ds(_aligned(b * SB, SB), SB), :], sem_z)

    @pl.when(i == 0)
    def _():
        for b in range(GLAG):
            in_copy(b).start()
        zbuf[...] = jnp.zeros_like(zbuf)

        def per_expert(ex, carry):
            def start_row(r, c2):
                zero_row(r).start()
                return c2
            lax.fori_loop(cend_ref[ex], pend_ref[ex], start_row, 0)
            return carry
        lax.fori_loop(0, N_EXPERTS, per_expert, 0)

        def start_blk(b, carry):
            zero_block(b).start()
            return carry
        lax.fori_loop(nblk_ref[0], n_sub, start_blk, 0)

    in_copy(i).wait()
    slot = i % GSLOTS
    for j in range(TC):
        for k in range(TOP_K):
            r = dest_ref[(i * TC + j) * TOP_K + k]
            pltpu.make_async_copy(ring.at[slot, pl.ds(j, 1), :], x_hbm.at[pl.ds(r, 1), :],
                                  sem_sc.at[slot]).start()

    @pl.when(i >= 1)
    def _():
        sc_wait(i - 1)

    @pl.when(i + GLAG < n)
    def _():
        in_copy(i + GLAG).start()

    @pl.when(i == n - 1)
    def _():
        sc_wait(i)

        def per_expert(ex, carry):
            def wait_row(r, c2):
                zero_row(0).wait()
                return c2
            lax.fori_loop(cend_ref[ex], pend_ref[ex], wait_row, 0)
            return carry
        lax.fori_loop(0, N_EXPERTS, per_expert, 0)

        def wait_blk(b, carry):
            zero_block(0).wait()
            return carry
        lax.fori_loop(nblk_ref[0], n_sub, wait_blk, 0)


def _scatter_rows(dest_flat, cnt_end, pad_end, nblk, h_words, n_pad):
    t = h_words.shape[0]
    any_spec = pl.BlockSpec(memory_space=pl.ANY)
    return pl.pallas_call(
        functools.partial(_scatter_rows_kernel, n_sub=n_pad // SB),
        out_shape=jax.ShapeDtypeStruct((n_pad, D_MODEL), F32),
        grid_spec=pltpu.PrefetchScalarGridSpec(
            num_scalar_prefetch=4,
            grid=(t // TC,),
            in_specs=[any_spec],
            out_specs=any_spec,
            scratch_shapes=[pltpu.VMEM((GSLOTS, TC, D_MODEL), F32),
                            pltpu.VMEM((SB, D_MODEL), F32),
                            pltpu.SemaphoreType.DMA((GSLOTS,)),
                            pltpu.SemaphoreType.DMA((GSLOTS,)),
                            pltpu.SemaphoreType.DMA]),
        compiler_params=pltpu.CompilerParams(dimension_semantics=("arbitrary",),
                                             vmem_limit_bytes=32 * MIB),
        name="scatter_rows",
    )(dest_flat, cnt_end, pad_end, nblk, h_words)


def _moe_kernel(ie_ref, irow_ref, insb_ref, nblk_ref,
                x_hbm, wg_hbm, wu_hbm, wd_hbm, bgu_ref, bd_ref,
                y_hbm,
                xring, x2d, act_buf, wdb, wgus, wds, wgub, ystage, sem_x, sem_w, sem_y, *, n_sub):
    g = pl.program_id(0)
    nsb = insb_ref[g]
    e = ie_ref[g]
    row0 = irow_ref[g]
    nsb_next = insb_ref[g + 1]
    row0_next = irow_ref[g + 1]
    e_next = ie_ref[g + 1]

    nsb_prev = insb_ref[jnp.maximum(g - 1, 0)]

    def x_copy(src_row, c):
        return pltpu.make_async_copy(x_hbm.at[pl.ds(_aligned(src_row + c * SB, SB), SB), :],
                                     xring.at[c % 2], sem_x.at[c % 2])

    def land(src_row, c, n_total):
        x_copy(src_row, c).wait()
        x2d[pl.ds(_aligned(c * SB, SB), SB), :] = xring[c % 2].astype(BF16)

        @pl.when(c + 2 < n_total)
        def _():
            x_copy(src_row, c + 2).start()

    def issue_first(src_row, n_total):
        for c in range(2):
            pl.when(c < n_total)(lambda c=c: x_copy(src_row, c).start())

    def w_copies(ex, f, slot):
        cols = pl.ds(_aligned(f * TF, TF), TF)
        return (pltpu.make_async_copy(wg_hbm.at[ex, :, cols], wgus.at[slot, 0], sem_w.at[slot, 0]),
                pltpu.make_async_copy(wu_hbm.at[ex, :, cols], wgus.at[slot, 1], sem_w.at[slot, 1]),
                pltpu.make_async_copy(wd_hbm.at[ex, cols, :], wds.at[slot], sem_w.at[slot, 2]))

    def y_copy(sb, slot):
        return pltpu.make_async_copy(ystage.at[slot],
                                     y_hbm.at[pl.ds(_aligned(row0 + sb * SB, SB), SB), :], sem_y.at[slot])

    def catch_up(c, carry):
        land(row0, c, nsb)
        return carry
    lax.fori_loop(jnp.minimum(nsb_prev, nsb), nsb, catch_up, 0)

    @pl.when(nsb == 0)
    def _():
        issue_first(row0_next, nsb_next)

    @pl.when(nsb > 0)
    def _():

        @pl.when(g == 1)
        def _():
            for cp in w_copies(e, 0, 0):
                cp.start()

        def gate_up_tile(f, carry):
            slot = f % 2

            @pl.when(f + 1 < NF)
            def _():
                for cp in w_copies(e, f + 1, 1 - slot):
                    cp.start()
            for cp in w_copies(e, f, slot):
                cp.wait()
            wgub[:, 0:TF] = wgus[slot, 0].astype(BF16)
            wgub[:, TF:2 * TF] = wgus[slot, 1].astype(BF16)
            wdb[pl.ds(_aligned(f * TF, TF), TF), :] = wds[slot].astype(BF16)
            b_gu = bgu_ref[pl.ds(e * NF + f, 1), :]

            def gate_up(row_start, m):
                rows = pl.ds(row_start, m)
                gu = jnp.dot(x2d[rows, :], wgub[...], preferred_element_type=F32) + b_gu
                gt = jnp.minimum(gu[:, 0:TF], SWIGLU_LIMIT)
                up = jnp.clip(gu[:, TF:2 * TF], -SWIGLU_LIMIT, SWIGLU_LIMIT)
                act = (up + 1.0) * (gt * jax.nn.sigmoid(SWIGLU_ALPHA * gt))
                act_buf[f, rows, :] = act.astype(BF16)

            for big in BIG_NSB:
                @pl.when(nsb == big)
                def _(big=big):
                    half = big * SB // 2
                    gate_up(0, half)
                    gate_up(half, half)

            @pl.when(functools.reduce(lambda a, b: a & b, [nsb != big for big in BIG_NSB]))
            def _():
                def one(sb, c2):
                    gate_up(_aligned(sb * SB, SB), SB)
                    return c2
                lax.fori_loop(0, nsb, one, 0)
            return carry
        lax.fori_loop(0, NF, gate_up_tile, 0)

        @pl.when(nsb_next > 0)
        def _():
            for cp in w_copies(e_next, 0, 0):
                cp.start()

        issue_first(row0_next, nsb_next)

        def down(sb, carry):
            yslot = sb % 2

            @pl.when(sb >= 2)
            def _():
                y_copy(sb - 2, yslot).wait()

            @pl.when(sb < nsb_next)
            def _():
                land(row0_next, sb, nsb_next)
            rows = pl.ds(_aligned(sb * SB, SB), SB)
            a = jnp.concatenate([act_buf[f, rows, :] for f in range(NF)], axis=1)
            ystage[yslot] = jnp.dot(a, wdb[...], preferred_element_type=F32) + bd_ref[pl.ds(e, 1), :]
            y_copy(sb, yslot).start()
            return carry
        lax.fori_loop(0, nsb, down, 0)
        y_copy(nsb - 1, (nsb - 1) % 2).wait()

        @pl.when(nsb >= 2)
        def _():
            y_copy(nsb - 2, nsb % 2).wait()

    @pl.when(g == pl.num_programs(0) - 1)
    def _():
        ystage[0] = jnp.zeros((SB, D_MODEL), F32)

        def fill(b):
            return pltpu.make_async_copy(ystage.at[0], y_hbm.at[pl.ds(_aligned(b * SB, SB), SB), :],
                                         sem_y.at[0])

        def start_fill(b, carry):
            fill(b).start()
            return carry

        def wait_fill(b, carry):
            fill(b).wait()
            return carry
        lax.fori_loop(nblk_ref[0], n_sub, start_fill, 0)
        lax.fori_loop(nblk_ref[0], n_sub, wait_fill, 0)


def _moe(item_e, item_row, item_nsb, nblk, x_sorted, w_gate, w_up, w_down, b_gate, b_up, b_down,
         n_items):
    any_spec = pl.BlockSpec(memory_space=pl.ANY)
    b_gu = jnp.concatenate([b_gate.reshape(N_EXPERTS * NF, TF), b_up.reshape(N_EXPERTS * NF, TF)], axis=1)
    grid_spec = pltpu.PrefetchScalarGridSpec(
        num_scalar_prefetch=4,
        grid=(n_items + 1,),
        in_specs=[any_spec, any_spec, any_spec, any_spec,
                  pl.BlockSpec((N_EXPERTS * NF, 2 * TF), lambda i, *_: (0, 0)),
                  pl.BlockSpec((N_EXPERTS, D_MODEL), lambda i, *_: (0, 0))],
        out_specs=any_spec,
        scratch_shapes=[pltpu.VMEM((2, SB, D_MODEL), F32),
                        pltpu.VMEM((RB, D_MODEL), BF16),
                        pltpu.VMEM((NF, RB, TF), BF16),
                        pltpu.VMEM((D_FF, D_MODEL), BF16),
                        pltpu.VMEM((2, 2, D_MODEL, TF), F32),
                        pltpu.VMEM((2, TF, D_MODEL), F32),
                        pltpu.VMEM((D_MODEL, 2 * TF), BF16),
                        pltpu.VMEM((2, SB, D_MODEL), F32),
                        pltpu.SemaphoreType.DMA((2,)),
                        pltpu.SemaphoreType.DMA((2, 3)),
                        pltpu.SemaphoreType.DMA((2,))],
    )
    return pl.pallas_call(
        functools.partial(_moe_kernel, n_sub=x_sorted.shape[0] // SB),
        out_shape=jax.ShapeDtypeStruct((x_sorted.shape[0], D_MODEL), F32),
        grid_spec=grid_spec,
        compiler_params=pltpu.CompilerParams(dimension_semantics=("arbitrary",),
                                             vmem_limit_bytes=56 * MIB),
        name="moe_experts",
    )(item_e, item_row, item_nsb, nblk, x_sorted, w_gate, w_up, w_down, b_gu, b_down)


def _combine_kernel(dest_ref, y_hbm, gate_ref, x1_ref, g2_ref, nf_ref, o_ref, ybuf, sem, *, tok_off):
    i = pl.program_id(0)
    n = pl.num_programs(0)

    def issue(b):
        slot = b % GSLOTS
        for j in range(TC):
            for k in range(TOP_K):
                r = dest_ref[(tok_off + b * TC + j) * TOP_K + k]
                pltpu.make_async_copy(y_hbm.at[pl.ds(r, 1), :], ybuf.at[slot, k, pl.ds(j, 1), :],
                                      sem.at[slot]).start()

    @pl.when(i == 0)
    def _():
        for b in range(GLAG):
            issue(b)

    pl.when(i + GLAG < n)(lambda: issue(i + GLAG))

    slot = i % GSLOTS
    for k in range(TOP_K):
        pltpu.make_async_copy(y_hbm.at[pl.ds(0, TC), :], ybuf.at[slot, k], sem.at[slot]).wait()
    gate = gate_ref[...]
    f = gate[:, 0:1] * ybuf[slot, 0]
    for k in range(1, TOP_K):
        f = f + gate[:, k:k + 1] * ybuf[slot, k]
    x2 = x1_ref[...] + g2_ref[...] * f
    o_ref[...] = _rms(x2) * nf_ref[...]


def _combine(dest, y_sorted, gates, x1, mod4, row_fn, norm_f, tok_off, t):
    off = tok_off // TC
    row_in = lambda w: pl.BlockSpec((TC, w), lambda i, *_: (off + i, 0))
    return pl.pallas_call(
        functools.partial(_combine_kernel, tok_off=tok_off),
        out_shape=jax.ShapeDtypeStruct((t, D_MODEL), F32),
        grid_spec=pltpu.PrefetchScalarGridSpec(
            num_scalar_prefetch=1,
            grid=(t // TC,),
            in_specs=[pl.BlockSpec(memory_space=pl.ANY), row_in(LANES), row_in(D_MODEL),
                      pl.BlockSpec((None, None, 1, D_MODEL),
                                   lambda i, *_: (row_fn(i * TC // TM), 5, 0, 0)),
                      pl.BlockSpec((1, D_MODEL), lambda i, *_: (0, 0))],
            out_specs=pl.BlockSpec((TC, D_MODEL), lambda i, *_: (i, 0)),
            scratch_shapes=[pltpu.VMEM((GSLOTS, TOP_K, TC, D_MODEL), F32),
                            pltpu.SemaphoreType.DMA((GSLOTS,))]),
        compiler_params=pltpu.CompilerParams(dimension_semantics=("arbitrary",),
                                             vmem_limit_bytes=40 * MIB),
        name="combine_norm",
    )(dest, y_sorted, gates, x1, mod4, norm_f)


def _rope_tables(n_tokens):
    n_rows = n_tokens // GRID_W
    row = jnp.repeat(jnp.arange(n_rows, dtype=F32), GRID_W)
    col = jnp.tile(jnp.arange(GRID_W, dtype=F32), n_rows)
    half = AXIS_DIM // 2
    inv_freq = ROPE_THETA ** (-jnp.arange(half, dtype=F32) / half)
    ang_r = row[:, None] * inv_freq[None, :]
    ang_c = col[:, None] * inv_freq[None, :]
    cos_t = jnp.concatenate([jnp.cos(ang_r)] * 2 + [jnp.cos(ang_c)] * 2, axis=-1)
    sin_t = jnp.concatenate([-jnp.sin(ang_r), jnp.sin(ang_r), -jnp.sin(ang_c), jnp.sin(ang_c)], axis=-1)
    return cos_t, sin_t


def kernel(x_prompt, x_sample, cache_k, cache_v, c, c_ctx, w_mod, b_mod, norm1, w_in, q_gain, k_gain,
           w_sp, b_sp, attn_out_gain, gmlp_out_gain, w_out, norm2, w_router, b_router,
           w_gate, b_gate, w_up, b_up, w_down, b_down, norm_f):
    bp, sp_len, _ = x_prompt.shape
    bs, ss_len, _ = x_sample.shape
    depth = w_mod.shape[0]
    n_p = bp * sp_len
    n_s = bs * ss_len
    t_total = n_p + n_s
    n_assign = t_total * TOP_K
    n_sub = (n_assign + N_EXPERTS * (SB - 1)) // SB + 1
    n_pad = n_sub * SB
    n_items = N_EXPERTS + -(-n_pad // RB)

    xp = x_prompt.reshape(n_p, D_MODEL)
    xs = x_sample.reshape(n_s, D_MODEL)
    tiles_per_sample = ss_len // TM
    row_p = lambda i: 0
    row_s = lambda i: 1 + i // tiles_per_sample
    row_all = lambda i: jnp.where(i < n_p // TM, 0, 1 + (i - n_p // TM) // tiles_per_sample)
    rope_tabs = _rope_tables(ss_len)
    cvec8 = jnp.zeros((8, D_MODEL), F32).at[0].set(c_ctx).at[1:1 + bs].set(c)
    new_k, new_v = [], []

    for l in range(depth):
        mod4 = _modulation(cvec8, w_mod[l], b_mod[l]).reshape(8, N_MOD, 1, D_MODEL)
        w_in_bf = w_in[l].astype(BF16)
        w_out_bf = w_out[l].astype(BF16)
        wsp_bf = w_sp[l].astype(BF16)
        bsp_b = jnp.broadcast_to(b_sp[l][:, :, None], (G_HEADS, CHUNK, G_DIM))
        n1 = norm1[l].reshape(1, D_MODEL)
        n2 = norm2[l].reshape(1, D_MODEL)
        qg = q_gain[l].reshape(1, HEAD_DIM)
        kg = k_gain[l].reshape(1, HEAD_DIM)
        gog = gmlp_out_gain[l].reshape(1, GMLP_W)
        aog = attn_out_gain[l].reshape(1, ATTN_W)
        wr_pad = jnp.zeros((D_MODEL, LANES), F32).at[:, :N_EXPERTS].set(w_router[l])
        br_pad = jnp.full((1, LANES), NEG, F32).at[0, :N_EXPERTS].set(b_router[l])

        qp, kp, vp, gmp, kpf, vpf = _inproj(xp, mod4, row_p, n1, w_in_bf, qg, kg, None, wsp_bf, bsp_b,
                                            gog, sp_len, True)
        qs, ks, vs, gms = _inproj(xs, mod4, row_s, n1, w_in_bf, qg, kg, rope_tabs, wsp_bf, bsp_b,
                                  gog, ss_len, False)
        new_k.append(kpf.reshape(bp, sp_len, KV_HEADS, HEAD_DIM))
        new_v.append(vpf.reshape(bp, sp_len, KV_HEADS, HEAD_DIM))

        attn_p = _attention(qp, kp.reshape(bp, sp_len, KV_W), vp.reshape(bp, sp_len, KV_W), aog, sp_len)
        k_all = jnp.concatenate([cache_k[:, l].reshape(bs, -1, KV_W).astype(BF16),
                                 ks.reshape(bs, ss_len, KV_W)], axis=1)
        v_all = jnp.concatenate([cache_v[:, l].reshape(bs, -1, KV_W).astype(BF16),
                                 vs.reshape(bs, ss_len, KV_W)], axis=1)
        attn_s = _attention(qs, k_all, v_all, aog, ss_len)

        x1, h2, idx_l, gates, rank_l, cnt = _outproj(attn_p, attn_s, gmp, gms, w_out_bf, xp, xs, mod4,
                                                     row_all, n2, wr_pad, br_pad)

        idx = idx_l[:, :TOP_K]
        rank = rank_l[:, :TOP_K]
        counts = cnt[0, :N_EXPERTS].astype(jnp.int32)
        padded = ((counts + SB - 1) // SB) * SB
        pad_end = jnp.cumsum(padded)
        pad_start = pad_end - padded
        dest = (pad_start[idx] + rank).astype(jnp.int32)
        dest_flat = dest.reshape(-1)
        items_e = (padded + RB - 1) // RB
        items_end = jnp.cumsum(items_e)
        j = jnp.arange(n_items, dtype=jnp.int32)
        item_e = jnp.minimum(jnp.sum((items_end[None, :] <= j[:, None]).astype(jnp.int32), axis=1),
                             N_EXPERTS - 1)
        within = j - (items_end - items_e)[item_e]
        item_row = pad_start[item_e] + within * RB
        item_rows = jnp.clip(padded[item_e] - within * RB, 0, RB)
        item_nsb = jnp.where(j < items_end[-1], item_rows // SB, 0).astype(jnp.int32)
        item_row = jnp.where(item_nsb > 0, item_row, 0).astype(jnp.int32)

        nblk = (pad_end[-1:] // SB).astype(jnp.int32)
        x_sorted = _scatter_rows(dest_flat, (pad_start + counts).astype(jnp.int32),
                                 pad_end.astype(jnp.int32), nblk, h2, n_pad)
        framed = lambda a: jnp.pad(a.astype(jnp.int32), (1, 1))
        y_sorted = _moe(framed(item_e), framed(item_row), framed(item_nsb), nblk, x_sorted,
                        w_gate[l], w_up[l], w_down[l], b_gate[l], b_up[l], b_down[l], n_items)

        nf = norm_f.reshape(1, D_MODEL)
        is_last = l == depth - 1
        assert is_last, "only DEPTH == 1 is supported"
        dest_flat = dest.reshape(-1)
        y_prompt = _combine(dest_flat, y_sorted, gates, x1, mod4, row_p, nf, 0, n_p)
        y_sample = _combine(dest_flat, y_sorted, gates, x1, mod4, row_s, nf, n_p, n_s)

    return (y_prompt.reshape(x_prompt.shape), y_sample.reshape(x_sample.shape),
            jnp.stack(new_k, axis=1), jnp.stack(new_v, axis=1))
```
